```python
import math
import jax, jax.numpy as jnp
from jax import lax
import numpy as np

D_MODEL = 1024
BATCH = 4
SEQ = 8192
DEPTH = 4

N_A_LAYERS = DEPTH // 2
N_B_LAYERS = DEPTH - N_A_LAYERS
D_FF = 2816
A_HEADS = 8
A_DK = 128
A_DV = 128
CONV_K = 4
CHUNK = 64
A_QKV = A_HEADS * (2 * A_DK + A_DV)
A_PROJ = A_QKV + A_HEADS * A_DV + 2 * A_HEADS
B_Q_HEADS = 16
B_KV_HEADS = 4
B_GROUP = B_Q_HEADS // B_KV_HEADS
B_DH = 64
WINDOW = 128
BLK = WINDOW
EPS = 1e-6

kernel_name = "yoco_deltanet_swa_sink_macaron"


def rmsnorm(x, g):
    x32 = x.astype(jnp.float32)
    y = x32 * lax.rsqrt(jnp.mean(x32 * x32, axis=-1, keepdims=True) + EPS)
    return (y * g.astype(jnp.float32)).astype(x.dtype)


def swiglu(h, w_gate, w_up, w_down):
    return (jax.nn.silu(h @ w_gate) * (h @ w_up)) @ w_down


def l2norm(x):
    return x * lax.rsqrt(jnp.sum(x * x, axis=-1, keepdims=True) + EPS)


def causal_conv_silu(x, w):
    k = w.shape[0]
    y = lax.conv_general_dilated(x, w[:, None, :].astype(x.dtype), window_strides=(1,),
                                 padding=[(k - 1, 0)], dimension_numbers=('NWC', 'WIO', 'NWC'),
                                 feature_group_count=x.shape[-1])
    return jax.nn.silu(y)


def gated_delta_rule(q, k, v, g, beta):
    b, l, h, dk = q.shape
    dv = v.shape[-1]
    n = l // CHUNK

    def to_chunks(t):
        t = t.reshape((b, n, CHUNK, h) + t.shape[3:])
        return jnp.moveaxis(t, 3, 2)

    q = to_chunks(l2norm(q) * dk ** -0.5)
    k = to_chunks(l2norm(k))
    v = to_chunks(v)
    beta = to_chunks(beta)
    g = jnp.cumsum(to_chunks(g), axis=-1)
    idx = jnp.arange(CHUNK)
    lower = idx[:, None] >= idx[None, :]
    strict = idx[:, None] > idx[None, :]
    decay = jnp.exp(jnp.where(lower, g[..., :, None] - g[..., None, :], -jnp.inf))
    k_beta = k * beta[..., None]
    a_mat = jnp.where(strict, jnp.einsum('bnhik,bnhjk->bnhij', k_beta, k) * decay, 0.0)
    rhs = jnp.concatenate([v * beta[..., None], k_beta * jnp.exp(g)[..., None]], axis=-1)
    sol = lax.linalg.triangular_solve(a_mat + jnp.eye(CHUNK, dtype=a_mat.dtype), rhs,
                                      left_side=True, lower=True, unit_diagonal=True)
    u, w = sol[..., :dv], sol[..., dv:]
    qk = jnp.einsum('bnhik,bnhjk->bnhij', q, k) * decay
    q_dec = q * jnp.exp(g)[..., None]
    k_dec = k * jnp.exp(g[..., -1:] - g)[..., None]
    g_last = jnp.exp(g[..., -1])

    def step(state, inp):
        q_i, k_i, u_i, w_i, qk_i, gl_i = inp
        v_new = u_i - jnp.einsum('bhck,bhkv->bhcv', w_i, state)
        o_i = jnp.einsum('bhck,bhkv->bhcv', q_i, state) + jnp.einsum('bhij,bhjv->bhiv', qk_i, v_new)
        state = state * gl_i[..., None, None] + jnp.einsum('bhck,bhcv->bhkv', k_i, v_new)
        return state, o_i

    xs = tuple(jnp.moveaxis(t, 1, 0) for t in (q_dec, k_dec, u, w, qk, g_last))
    s0 = jnp.zeros((b, h, dk, dv), jnp.float32)
    _, o = lax.scan(step, s0, xs)
    return o.transpose(1, 0, 3, 2, 4).reshape(b, l, h, dv)


def deltanet_mixer(h, w_in, conv_w, a_log, dt_bias, o_norm, w_out):
    b, l, _ = h.shape
    f32 = jnp.float32
    proj = h @ w_in
    qkv = causal_conv_silu(proj[..., :A_QKV], conv_w)
    q, k, v = jnp.split(qkv, [A_HEADS * A_DK, 2 * A_HEADS * A_DK], axis=-1)
    gate = proj[..., A_QKV:A_QKV + A_HEADS * A_DV].reshape(b, l, A_HEADS, A_DV)
    b_raw = proj[..., A_QKV + A_HEADS * A_DV:A_QKV + A_HEADS * A_DV + A_HEADS]
    a_raw = proj[..., A_PROJ - A_HEADS:]
    beta = jax.nn.sigmoid(b_raw.astype(f32))
    g = -jnp.exp(a_log.astype(f32)) * jax.nn.softplus(a_raw.astype(f32) + dt_bias.astype(f32))
    o = gated_delta_rule(q.reshape(b, l, A_HEADS, A_DK).astype(f32),
                         k.reshape(b, l, A_HEADS, A_DK).astype(f32),
                         v.reshape(b, l, A_HEADS, A_DV).astype(f32), g, beta)
    o = rmsnorm(o, o_norm) * jax.nn.silu(gate.astype(f32))
    return o.reshape(b, l, A_HEADS * A_DV).astype(h.dtype) @ w_out


def shared_kv_blocks(x, kv_norm, kv_w, kv_b):
    b, l, _ = x.shape
    nb = l // BLK
    kv = rmsnorm(x, kv_norm) @ kv_w + kv_b
    k, v = jnp.split(kv, 2, axis=-1)

    def band(t):
        t = t.reshape(b, nb, BLK, B_KV_HEADS, B_DH)
        prev = jnp.concatenate([jnp.zeros_like(t[:, :1]), t[:, :-1]], axis=1)
        return jnp.concatenate([prev, t], axis=2)

    return band(k), band(v)


def window_mask(nb):
    qi = jnp.arange(BLK)[:, None]
    ki = jnp.arange(2 * BLK)[None, :]
    in_win = (ki > qi) & (ki <= qi + WINDOW)
    real = (jnp.arange(nb)[:, None, None] > 0) | (ki >= BLK)[None]
    return in_win[None] & real


def swa_sink_mixer(h, w_q, b_q, sinks, w_out, k_blk, v_blk):
    b, l, _ = h.shape
    nb = l // BLK
    q = (h @ w_q + b_q).reshape(b, nb, BLK, B_KV_HEADS, B_GROUP, B_DH)
    s = jnp.einsum('bnqhgd,bnkhd->bnhgqk', q, k_blk).astype(jnp.float32) * B_DH ** -0.5
    s = jnp.where(window_mask(nb)[None, :, None, None], s, -jnp.inf)
    sink = sinks.astype(jnp.float32).reshape(1, 1, B_KV_HEADS, B_GROUP, 1, 1)
    m = jnp.maximum(jnp.max(s, axis=-1, keepdims=True), sink)
    p = jnp.exp(s - m)
    p = p / (jnp.sum(p, axis=-1, keepdims=True) + jnp.exp(sink - m))
    o = jnp.einsum('bnhgqk,bnkhd->bnqhgd', p.astype(v_blk.dtype), v_blk)
    return o.reshape(b, l, B_Q_HEADS * B_DH) @ w_out


def setup_inputs(seed: int = 0) -> dict:
    key = jax.random.key(seed)
    ks = jax.random.split(key, 20)
    f32 = jnp.float32

    def nrm(k, shape, fan_in):
        return jax.random.normal(k, shape, f32) * fan_in ** -0.5

    def gain(k, shape):
        return 1.0 + 0.02 * jax.random.normal(k, shape, f32)

    x = jax.random.normal(ks[0], (BATCH, SEQ, D_MODEL), f32)
    ffn_norm = gain(ks[1], (DEPTH, 2, D_MODEL))
    ffn_w_gate = nrm(ks[2], (DEPTH, 2, D_MODEL, D_FF), D_MODEL)
    ffn_w_up = nrm(ks[3], (DEPTH, 2, D_MODEL, D_FF), D_MODEL)
    ffn_w_down = nrm(ks[4], (DEPTH, 2, D_FF, D_MODEL), D_FF)
    mix_norm = gain(ks[5], (DEPTH, D_MODEL))
    a_w_in = nrm(ks[6], (N_A_LAYERS, D_MODEL, A_PROJ), D_MODEL)
    a_conv = nrm(ks[7], (N_A_LAYERS, CONV_K, A_QKV), CONV_K)
    a_A_log = jnp.log(jax.random.uniform(ks[8], (N_A_LAYERS, A_HEADS), f32, 1.0, 16.0))
    dt = jnp.exp(jax.random.uniform(ks[9], (N_A_LAYERS, A_HEADS), f32, math.log(1e-3), math.log(1e-1)))
    a_dt_bias = dt + jnp.log(-jnp.expm1(-dt))
    a_o_norm = gain(ks[10], (N_A_LAYERS, A_DV))
    a_w_out = nrm(ks[11], (N_A_LAYERS, A_HEADS * A_DV, D_MODEL), A_HEADS * A_DV)
    kv_norm = gain(ks[12], (D_MODEL,))
    kv_w = nrm(ks[13], (D_MODEL, 2 * B_KV_HEADS * B_DH), D_MODEL)
    kv_b = 0.02 * jax.random.normal(ks[14], (2 * B_KV_HEADS * B_DH,), f32)
    b_w_q = nrm(ks[15], (N_B_LAYERS, D_MODEL, B_Q_HEADS * B_DH), D_MODEL)
    b_b_q = 0.02 * jax.random.normal(ks[16], (N_B_LAYERS, B_Q_HEADS * B_DH), f32)
    b_sinks = 0.5 * jax.random.normal(ks[17], (N_B_LAYERS, B_Q_HEADS), f32)
    b_w_out = nrm(ks[18], (N_B_LAYERS, B_Q_HEADS * B_DH, D_MODEL), B_Q_HEADS * B_DH)
    final_norm = gain(ks[19], (D_MODEL,))
    return {"x": x, "ffn_norm": ffn_norm, "ffn_w_gate": ffn_w_gate, "ffn_w_up": ffn_w_up,
            "ffn_w_down": ffn_w_down, "mix_norm": mix_norm, "a_w_in": a_w_in, "a_conv": a_conv,
            "a_A_log": a_A_log, "a_dt_bias": a_dt_bias, "a_o_norm": a_o_norm, "a_w_out": a_w_out,
            "kv_norm": kv_norm, "kv_w": kv_w, "kv_b": kv_b, "b_w_q": b_w_q, "b_b_q": b_b_q,
            "b_sinks": b_sinks, "b_w_out": b_w_out, "final_norm": final_norm}


def reference(x, ffn_norm, ffn_w_gate, ffn_w_up, ffn_w_down, mix_norm, a_w_in, a_conv, a_A_log,
              a_dt_bias, a_o_norm, a_w_out, kv_norm, kv_w, kv_b, b_w_q, b_b_q, b_sinks, b_w_out,
              final_norm):
    k_blk = v_blk = None
    for i in range(DEPTH):
        x = x + 0.5 * swiglu(rmsnorm(x, ffn_norm[i, 0]), ffn_w_gate[i, 0], ffn_w_up[i, 0], ffn_w_down[i, 0])
        h = rmsnorm(x, mix_norm[i])
        if i < N_A_LAYERS:
            x = x + deltanet_mixer(h, a_w_in[i], a_conv[i], a_A_log[i], a_dt_bias[i], a_o_norm[i], a_w_out[i])
        else:
            j = i - N_A_LAYERS
            x = x + swa_sink_mixer(h, b_w_q[j], b_b_q[j], b_sinks[j], b_w_out[j], k_blk, v_blk)
        x = x + 0.5 * swiglu(rmsnorm(x, ffn_norm[i, 1]), ffn_w_gate[i, 1], ffn_w_up[i, 1], ffn_w_down[i, 1])
        if i == N_A_LAYERS - 1:
            k_blk, v_blk = shared_kv_blocks(x, kv_norm, kv_w, kv_b)
    return rmsnorm(x, final_norm)
```

```python
import functools
import math

import jax
import jax.numpy as jnp
from jax import lax
from jax.experimental import pallas as pl
from jax.experimental.pallas import tpu as pltpu

F32 = jnp.float32
BF16 = jnp.bfloat16
HIGHEST = lax.Precision.HIGHEST

D_MODEL = 1024
D_FF = 2816
DEPTH = 4
N_A_LAYERS = 2
A_HEADS = 8
A_DK = 128
A_DV = 128
CONV_K = 4
CHUNK = 64
A_QKV = A_HEADS * (2 * A_DK + A_DV)
A_MAIN = A_QKV + A_HEADS * A_DV
B_Q_HEADS = 16
B_KV_HEADS = 4
B_GROUP = B_Q_HEADS // B_KV_HEADS
B_DH = 64
WINDOW = 128
EPS = 1e-6

LANES = 128
CONV_HALO = 8
VMEM_LIMIT = 56 * 1024 * 1024

FFN_ROWS = 512
PROJ_ROWS = 512
SWA_ROWS = 256


def _rms(x, g):
    return x * lax.rsqrt(jnp.mean(x * x, axis=-1, keepdims=True) + EPS) * g


def _silu(x):
    return x / (1.0 + jnp.exp(-x))


def _const_spec(shape):
    return pl.BlockSpec(shape, lambda *_: (0,) * len(shape), pipeline_mode=pl.Buffered(1))


def _params(*semantics):
    return pltpu.CompilerParams(dimension_semantics=semantics, vmem_limit_bytes=VMEM_LIMIT)


def _ffn_body(x_ref, g_ref, wg_ref, wu_ref, wd_ref, fg_ref, o_ref, *, final):
    x = x_ref[...]
    h = _rms(x, g_ref[...]).astype(BF16)
    gate = jnp.dot(h, wg_ref[...], preferred_element_type=F32)
    up = jnp.dot(h, wu_ref[...], preferred_element_type=F32)
    act = (_silu(gate) * up).astype(BF16)
    out = x + 0.5 * jnp.dot(act, wd_ref[...], preferred_element_type=F32)
    if final:
        out = _rms(out, fg_ref[...])
    o_ref[...] = out


def _ffn(x, g, wg, wu, wd, final_gain=None):
    t = x.shape[0]
    final = final_gain is not None
    fg = final_gain if final else g
    row = pl.BlockSpec((FFN_ROWS, D_MODEL), lambda i: (i, 0))
    return pl.pallas_call(
        functools.partial(_ffn_body, final=final),
        grid=(t // FFN_ROWS,),
        in_specs=[row, _const_spec((1, D_MODEL)), _const_spec((D_MODEL, D_FF)), _const_spec((D_MODEL, D_FF)),
                  _const_spec((D_FF, D_MODEL)), _const_spec((1, D_MODEL))],
        out_specs=row,
        out_shape=jax.ShapeDtypeStruct((t, D_MODEL), F32),
        compiler_params=_params("parallel"),
        name="ffn",
    )(x, g.reshape(1, D_MODEL), wg.astype(BF16), wu.astype(BF16), wd.astype(BF16), fg.reshape(1, D_MODEL))


def _a_in_body(x_ref, g_ref, wm_ref, wb_ref, main_ref, ba_ref):
    h = _rms(x_ref[...], g_ref[...]).astype(BF16)
    main_ref[...] = jnp.dot(h, wm_ref[...], preferred_element_type=F32)
    ba_ref[...] = jnp.dot(h, wb_ref[...], preferred_element_type=F32)


def _a_in_proj(x, g, w_in):
    t = x.shape[0]
    w_main = w_in[:, :A_MAIN].astype(BF16)
    w_ba = jnp.pad(w_in[:, A_MAIN:], ((0, 0), (0, LANES - 2 * A_HEADS))).astype(BF16)
    return pl.pallas_call(
        _a_in_body,
        grid=(t // PROJ_ROWS,),
        in_specs=[pl.BlockSpec((PROJ_ROWS, D_MODEL), lambda i: (i, 0)), _const_spec((1, D_MODEL)),
                  _const_spec((D_MODEL, A_MAIN)), _const_spec((D_MODEL, LANES))],
        out_specs=[pl.BlockSpec((PROJ_ROWS, A_MAIN), lambda i: (i, 0)),
                   pl.BlockSpec((PROJ_ROWS, LANES), lambda i: (i, 0))],
        out_shape=[jax.ShapeDtypeStruct((t, A_MAIN), F32), jax.ShapeDtypeStruct((t, LANES), F32)],
        compiler_params=_params("parallel"),
        name="a_in_proj",
    )(x, g.reshape(1, D_MODEL), w_main, w_ba)


def _out_proj_body(x_ref, o_ref, w_ref, y_ref):
    y_ref[...] = x_ref[...] + jnp.dot(o_ref[...], w_ref[...], preferred_element_type=F32)


def _out_proj(x, o, w):
    t = x.shape[0]
    row = pl.BlockSpec((PROJ_ROWS, D_MODEL), lambda i: (i, 0))
    return pl.pallas_call(
        _out_proj_body,
        grid=(t // PROJ_ROWS,),
        in_specs=[row, pl.BlockSpec((PROJ_ROWS, w.shape[0]), lambda i: (i, 0)), _const_spec(w.shape)],
        out_specs=row,
        out_shape=jax.ShapeDtypeStruct((t, D_MODEL), F32),
        compiler_params=_params("parallel"),
        name="out_proj",
    )(x, o, w.astype(BF16))


def _unit_lower_inverse(a, strict):
    c = a.shape[0]
    row = lax.broadcasted_iota(jnp.int32, (c, c), 0)
    col = lax.broadcasted_iota(jnp.int32, (c, c), 1)
    eye = jnp.where(row == col, 1.0, 0.0).astype(F32)
    p = jnp.where(strict, -a, 0.0)
    t = eye + p
    k = 1
    while 2 * k < c:
        p = jnp.dot(p, p, precision=HIGHEST, preferred_element_type=F32)
        t = t + jnp.dot(t, p, precision=HIGHEST, preferred_element_type=F32)
        k *= 2
    return t


def _delta_body(main_ref, ba_ref, cw_ref, alog_ref, dtb_ref, onorm_ref, o_ref, hist_ref, state_ref):
    c = CHUNK

    @pl.when(pl.program_id(1) == 0)
    def _():
        hist_ref[0:CONV_HALO, :] = jnp.zeros((CONV_HALO, A_QKV), F32)
        state_ref[...] = jnp.zeros(state_ref.shape, F32)

    hist_ref[CONV_HALO:CONV_HALO + c, :] = main_ref[:, 0:A_QKV]
    conv = cw_ref[CONV_K - 1:CONV_K, :] * hist_ref[CONV_HALO:CONV_HALO + c, :]
    for j in range(CONV_K - 1):
        lag = CONV_K - 1 - j
        conv = conv + cw_ref[j:j + 1, :] * hist_ref[CONV_HALO - lag:CONV_HALO - lag + c, :]
    hist_ref[0:CONV_HALO, :] = hist_ref[c:c + CONV_HALO, :]
    qkv = _silu(conv)

    ba = ba_ref[...]
    beta_all = 1.0 / (1.0 + jnp.exp(-ba))
    z = ba + dtb_ref[...]
    softplus = jnp.maximum(z, 0.0) + jnp.log(1.0 + jnp.exp(-jnp.abs(z)))
    g_all = -jnp.exp(alog_ref[...]) * softplus

    row = lax.broadcasted_iota(jnp.int32, (c, c), 0)
    col = lax.broadcasted_iota(jnp.int32, (c, c), 1)
    lower = row >= col
    strict = row > col
    tri = jnp.where(lower, 1.0, 0.0).astype(F32)
    ones = jnp.ones((c, c), F32)
    gcum_all = jnp.dot(tri, g_all, precision=HIGHEST, preferred_element_type=F32)

    for h in range(A_HEADS):
        q = qkv[:, h * A_DK:(h + 1) * A_DK]
        k = qkv[:, A_HEADS * A_DK + h * A_DK:A_HEADS * A_DK + (h + 1) * A_DK]
        v = qkv[:, 2 * A_HEADS * A_DK + h * A_DV:2 * A_HEADS * A_DK + (h + 1) * A_DV]
        q = q * lax.rsqrt(jnp.sum(q * q, axis=-1, keepdims=True) + EPS) * (A_DK ** -0.5)
        k = k * lax.rsqrt(jnp.sum(k * k, axis=-1, keepdims=True) + EPS)
        beta = beta_all[:, h:h + 1]
        gcol = gcum_all[:, A_HEADS + h:A_HEADS + h + 1]
        gcol_b = jnp.broadcast_to(gcol, (c, c))
        grow = jnp.dot(ones, jnp.where(row == col, gcol_b, 0.0), precision=HIGHEST, preferred_element_type=F32)
        decay = jnp.exp(jnp.where(lower, gcol_b - grow, -jnp.inf))
        g_last = gcol[c - 1:c, :]
        e_g = jnp.exp(gcol)
        k_beta = k * beta
        kk = lax.dot_general(k_beta, k, (((1,), (1,)), ((), ())), precision=HIGHEST, preferred_element_type=F32)
        t_inv = _unit_lower_inverse(kk * decay, strict)
        rhs = jnp.concatenate([v * beta, k_beta * e_g], axis=-1)
        sol = jnp.dot(t_inv, rhs, precision=HIGHEST, preferred_element_type=F32)
        u = sol[:, :A_DV]
        w = sol[:, A_DV:]
        qk = lax.dot_general(q.astype(BF16), k.astype(BF16), (((1,), (1,)), ((), ())),
                             preferred_element_type=F32) * decay
        q_dec = q * e_g
        k_dec = k * jnp.exp(g_last - gcol)

        s = state_ref[h]
        sb = s.astype(BF16)
        v_new = u - jnp.dot(w.astype(BF16), sb, preferred_element_type=F32)
        o = (jnp.dot(q_dec.astype(BF16), sb, preferred_element_type=F32)
             + jnp.dot(qk.astype(BF16), v_new.astype(BF16), preferred_element_type=F32))
        state_ref[h] = s * jnp.exp(g_last) + lax.dot_general(
            k_dec.astype(BF16), v_new.astype(BF16), (((0,), (0,)), ((), ())), preferred_element_type=F32)

        gate = main_ref[:, A_QKV + h * A_DV:A_QKV + (h + 1) * A_DV]
        o_ref[:, h * A_DV:(h + 1) * A_DV] = (_rms(o, onorm_ref[...]) * _silu(gate)).astype(BF16)


def _delta_mixer_core(main, ba, conv_w, a_log, dt_bias, o_norm, batch, seq):
    n_chunks = seq // CHUNK
    lane_pad = lambda p, lo: jnp.pad(p, (lo, LANES - lo - A_HEADS)).reshape(1, LANES)
    rows = lambda b, n: (b * n_chunks + n, 0)
    return pl.pallas_call(
        _delta_body,
        grid=(batch, n_chunks),
        in_specs=[pl.BlockSpec((CHUNK, A_MAIN), rows), pl.BlockSpec((CHUNK, LANES), rows),
                  _const_spec((CONV_K, A_QKV)), _const_spec((1, LANES)), _const_spec((1, LANES)),
                  _const_spec((1, A_DV))],
        out_specs=pl.BlockSpec((CHUNK, A_HEADS * A_DV), rows),
        out_shape=jax.ShapeDtypeStruct((batch * seq, A_HEADS * A_DV), BF16),
        scratch_shapes=[pltpu.VMEM((CHUNK + CONV_HALO, A_QKV), F32),
                        pltpu.VMEM((A_HEADS, A_DK, A_DV), F32)],
        compiler_params=_params("parallel", "arbitrary"),
        name="delta_rule",
    )(main, ba, conv_w, lane_pad(a_log, A_HEADS), lane_pad(dt_bias, A_HEADS), o_norm.reshape(1, A_DV))


def _kv_body(x_ref, g_ref, w_ref, b_ref, kv_ref):
    h = _rms(x_ref[...], g_ref[...]).astype(BF16)
    kv_ref[...] = (jnp.dot(h, w_ref[...], preferred_element_type=F32) + b_ref[...]).astype(BF16)


def _shared_kv(x, g, w, b):
    t = x.shape[0]
    n = w.shape[1]
    return pl.pallas_call(
        _kv_body,
        grid=(t // PROJ_ROWS,),
        in_specs=[pl.BlockSpec((PROJ_ROWS, D_MODEL), lambda i: (i, 0)), _const_spec((1, D_MODEL)),
                  _const_spec((D_MODEL, n)), _const_spec((1, n))],
        out_specs=pl.BlockSpec((PROJ_ROWS, n), lambda i: (i, 0)),
        out_shape=jax.ShapeDtypeStruct((t, n), BF16),
        compiler_params=_params("parallel"),
        name="shared_kv",
    )(x, g.reshape(1, D_MODEL), w.astype(BF16), b.reshape(1, n))


def _swa_body(sink_ref, x_ref, g_ref, wq_ref, bq_ref, kvp_ref, kvc_ref, wo_ref, y_ref, q_ref, o_ref):
    first_tile = pl.program_id(1) == 0
    kv_half = B_KV_HEADS * B_DH
    x = x_ref[...]
    h = _rms(x, g_ref[...]).astype(BF16)
    q_ref[...] = ((jnp.dot(h, wq_ref[...], preferred_element_type=F32) + bq_ref[...]) * (B_DH ** -0.5)).astype(BF16)

    qi = lax.broadcasted_iota(jnp.int32, (WINDOW, 2 * WINDOW), 0)
    ki = lax.broadcasted_iota(jnp.int32, (WINDOW, 2 * WINDOW), 1)
    in_win = (ki > qi) & (ki <= qi + WINDOW)
    for sub in range(SWA_ROWS // WINDOW):
        r0 = sub * WINDOW
        if sub == 0:
            keys = jnp.concatenate([kvp_ref[...], kvc_ref[0:WINDOW, :]], axis=0)
            mask = in_win & (jnp.logical_not(first_tile) | (ki >= WINDOW))
        else:
            keys = kvc_ref[r0 - WINDOW:r0 + WINDOW, :]
            mask = in_win
        for hk in range(B_KV_HEADS):
            kh = keys[:, hk * B_DH:(hk + 1) * B_DH]
            vh = keys[:, kv_half + hk * B_DH:kv_half + (hk + 1) * B_DH]
            for g in range(B_GROUP):
                hq = hk * B_GROUP + g
                sink = sink_ref[hq]
                s = lax.dot_general(q_ref[r0:r0 + WINDOW, hq * B_DH:(hq + 1) * B_DH], kh,
                                    (((1,), (1,)), ((), ())), preferred_element_type=F32)
                s = jnp.where(mask, s, -jnp.inf)
                m = jnp.maximum(jnp.max(s, axis=-1, keepdims=True), sink)
                p = jnp.exp(s - m)
                denom = jnp.sum(p, axis=-1, keepdims=True) + jnp.exp(sink - m)
                o = jnp.dot(p.astype(BF16), vh, preferred_element_type=F32) / denom
                o_ref[r0:r0 + WINDOW, hq * B_DH:(hq + 1) * B_DH] = o.astype(BF16)
    y_ref[...] = x + jnp.dot(o_ref[...], wo_ref[...], preferred_element_type=F32)


def _swa_mixer(x, g, w_q, b_q, sinks, w_out, kv, batch, seq):
    tiles = seq // SWA_ROWS
    per_tile = SWA_ROWS // WINDOW
    dq = B_Q_HEADS * B_DH
    row = pl.BlockSpec((SWA_ROWS, D_MODEL), lambda b, i: (b * tiles + i, 0))
    prev = pl.BlockSpec((WINDOW, kv.shape[1]),
                        lambda b, i: ((b * tiles + i) * per_tile - jnp.minimum(i, 1), 0))
    cur = pl.BlockSpec((SWA_ROWS, kv.shape[1]), lambda b, i: (b * tiles + i, 0))
    return pl.pallas_call(
        _swa_body,
        grid=(batch, tiles),
        in_specs=[pl.BlockSpec(memory_space=pltpu.SMEM), row, _const_spec((1, D_MODEL)),
                  _const_spec((D_MODEL, dq)), _const_spec((1, dq)), prev, cur, _const_spec((dq, D_MODEL))],
        out_specs=row,
        out_shape=jax.ShapeDtypeStruct((batch * seq, D_MODEL), F32),
        scratch_shapes=[pltpu.VMEM((SWA_ROWS, dq), BF16), pltpu.VMEM((SWA_ROWS, dq), BF16)],
        compiler_params=_params("parallel", "arbitrary"),
        name="swa_mixer",
    )(sinks, x, g.reshape(1, D_MODEL), w_q.astype(BF16), b_q.reshape(1, dq), kv, kv, w_out.astype(BF16))


def kernel(x, ffn_norm, ffn_w_gate, ffn_w_up, ffn_w_down, mix_norm, a_w_in, a_conv, a_A_log, a_dt_bias,
           a_o_norm, a_w_out, kv_norm, kv_w, kv_b, b_w_q, b_b_q, b_sinks, b_w_out, final_norm):
    batch, seq, d = x.shape
    x = x.reshape(batch * seq, d)
    kv = None
    for i in range(DEPTH):
        x = _ffn(x, ffn_norm[i, 0], ffn_w_gate[i, 0], ffn_w_up[i, 0], ffn_w_down[i, 0])
        if i < N_A_LAYERS:
            main, ba = _a_in_proj(x, mix_norm[i], a_w_in[i])
            o = _delta_mixer_core(main, ba, a_conv[i], a_A_log[i], a_dt_bias[i], a_o_norm[i], batch, seq)
            x = _out_proj(x, o, a_w_out[i])
        else:
            j = i - N_A_LAYERS
            x = _swa_mixer(x, mix_norm[i], b_w_q[j], b_b_q[j], b_sinks[j], b_w_out[j], kv, batch, seq)
        last = i == DEPTH - 1
        x = _ffn(x, ffn_norm[i, 1], ffn_w_gate[i, 1], ffn_w_up[i, 1], ffn_w_down[i, 1],
                 final_gain=final_norm if last else None)
        if i == N_A_LAYERS - 1:
            kv = _shared_kv(x, kv_norm, kv_w, kv_b)
    return x.reshape(batch, seq, d)
```

```python
import functools
import math

import jax
import jax.numpy as jnp
from jax import lax
from jax.experimental import pallas as pl
from jax.experimental.pallas import tpu as pltpu

F32 = jnp.float32
BF16 = jnp.bfloat16
HIGHEST = lax.Precision.HIGHEST

D_MODEL = 1024
D_FF = 2816
DEPTH = 4
N_A_LAYERS = 2
A_HEADS = 8
A_DK = 128
A_DV = 128
CONV_K = 4
CHUNK = 128
INV_BASE = 8
A_QKV = A_HEADS * (2 * A_DK + A_DV)
A_MAIN = A_QKV + A_HEADS * A_DV
B_Q_HEADS = 16
B_KV_HEADS = 4
B_GROUP = B_Q_HEADS // B_KV_HEADS
B_DH = 64
WINDOW = 128
EPS = 1e-6

LANES = 128
CONV_HALO = 8
VMEM_LIMIT = 56 * 1024 * 1024

FFN_ROWS = 512
PROJ_ROWS = 512
SWA_ROWS = 256


def _rms(x, g):
    return x * lax.rsqrt(jnp.mean(x * x, axis=-1, keepdims=True) + EPS) * g


def _silu(x):
    return x / (1.0 + jnp.exp(-x))


def _const_spec(shape):
    return pl.BlockSpec(shape, lambda *_: (0,) * len(shape), pipeline_mode=pl.Buffered(1))


def _params(*semantics):
    return pltpu.CompilerParams(dimension_semantics=semantics, vmem_limit_bytes=VMEM_LIMIT)


def _ffn_body(x_ref, g_ref, wg_ref, wu_ref, wd_ref, fg_ref, o_ref, *, final):
    x = x_ref[...]
    h = _rms(x, g_ref[...]).astype(BF16)
    gate = jnp.dot(h, wg_ref[...], preferred_element_type=F32)
    up = jnp.dot(h, wu_ref[...], preferred_element_type=F32)
    act = (_silu(gate) * up).astype(BF16)
    out = x + 0.5 * jnp.dot(act, wd_ref[...], preferred_element_type=F32)
    if final:
        out = _rms(out, fg_ref[...])
    o_ref[...] = out


def _ffn(x, g, wg, wu, wd, final_gain=None):
    t = x.shape[0]
    final = final_gain is not None
    fg = final_gain if final else g
    row = pl.BlockSpec((FFN_ROWS, D_MODEL), lambda i: (i, 0))
    return pl.pallas_call(
        functools.partial(_ffn_body, final=final),
        grid=(t // FFN_ROWS,),
        in_specs=[row, _const_spec((1, D_MODEL)), _const_spec((D_MODEL, D_FF)), _const_spec((D_MODEL, D_FF)),
                  _const_spec((D_FF, D_MODEL)), _const_spec((1, D_MODEL))],
        out_specs=row,
        out_shape=jax.ShapeDtypeStruct((t, D_MODEL), F32),
        compiler_params=_params("parallel"),
        name="ffn",
    )(x, g.reshape(1, D_MODEL), wg.astype(BF16), wu.astype(BF16), wd.astype(BF16), fg.reshape(1, D_MODEL))


def _a_in_body(x_ref, g_ref, wm_ref, wb_ref, main_ref, ba_ref):
    h = _rms(x_ref[...], g_ref[...]).astype(BF16)
    main_ref[...] = jnp.dot(h, wm_ref[...], preferred_element_type=F32)
    ba_ref[...] = jnp.dot(h, wb_ref[...], preferred_element_type=F32)


def _a_in_proj(x, g, w_in):
    t = x.shape[0]
    w_main = w_in[:, :A_MAIN].astype(BF16)
    w_ba = jnp.pad(w_in[:, A_MAIN:], ((0, 0), (0, LANES - 2 * A_HEADS))).astype(BF16)
    return pl.pallas_call(
        _a_in_body,
        grid=(t // PROJ_ROWS,),
        in_specs=[pl.BlockSpec((PROJ_ROWS, D_MODEL), lambda i: (i, 0)), _const_spec((1, D_MODEL)),
                  _const_spec((D_MODEL, A_MAIN)), _const_spec((D_MODEL, LANES))],
        out_specs=[pl.BlockSpec((PROJ_ROWS, A_MAIN), lambda i: (i, 0)),
                   pl.BlockSpec((PROJ_ROWS, LANES), lambda i: (i, 0))],
        out_shape=[jax.ShapeDtypeStruct((t, A_MAIN), F32), jax.ShapeDtypeStruct((t, LANES), F32)],
        compiler_params=_params("parallel"),
        name="a_in_proj",
    )(x, g.reshape(1, D_MODEL), w_main, w_ba)


def _out_proj_body(x_ref, o_ref, w_ref, y_ref):
    y_ref[...] = x_ref[...] + jnp.dot(o_ref[...], w_ref[...], preferred_element_type=F32)


def _out_proj(x, o, w):
    t = x.shape[0]
    row = pl.BlockSpec((PROJ_ROWS, D_MODEL), lambda i: (i, 0))
    return pl.pallas_call(
        _out_proj_body,
        grid=(t // PROJ_ROWS,),
        in_specs=[row, pl.BlockSpec((PROJ_ROWS, w.shape[0]), lambda i: (i, 0)), _const_spec(w.shape)],
        out_specs=row,
        out_shape=jax.ShapeDtypeStruct((t, D_MODEL), F32),
        compiler_params=_params("parallel"),
        name="out_proj",
    )(x, o, w.astype(BF16))


def _mm(a, b):
    return jnp.dot(a.astype(BF16), b.astype(BF16), preferred_element_type=F32)


def _mm_nt(a, b):
    return lax.dot_general(a.astype(BF16), b.astype(BF16), (((1,), (1,)), ((), ())), preferred_element_type=F32)


def _unit_lower_inverses(mats, xor):
    c = mats[0].shape[0]
    ps = [jnp.where(xor < INV_BASE, -a, 0.0) for a in mats]
    ts = [jnp.where(xor == 0, 1.0, 0.0) + p for p in ps]
    k = 2
    while k < INV_BASE:
        ps = [_mm(p, p) for p in ps]
        ts = [t + _mm(t, p) for t, p in zip(ts, ps)]
        k *= 2
    s = INV_BASE
    while s < c:
        off = (xor >= s) & (xor < 2 * s)
        ats = [_mm(jnp.where(off, a, 0.0), t) for a, t in zip(mats, ts)]
        ts = [t - _mm(t, at) for t, at in zip(ts, ats)]
        s *= 2
    return ts


def _delta_body(main_ref, ba_ref, cw_ref, alog_ref, dtb_ref, onorm_ref, o_ref, hist_ref, state_ref):
    c = CHUNK

    @pl.when(pl.program_id(1) == 0)
    def _():
        hist_ref[0:CONV_HALO, :] = jnp.zeros((CONV_HALO, A_QKV), F32)
        state_ref[...] = jnp.zeros(state_ref.shape, F32)

    hist_ref[CONV_HALO:CONV_HALO + c, :] = main_ref[:, 0:A_QKV]
    conv = cw_ref[CONV_K - 1:CONV_K, :] * hist_ref[CONV_HALO:CONV_HALO + c, :]
    for j in range(CONV_K - 1):
        lag = CONV_K - 1 - j
        conv = conv + cw_ref[j:j + 1, :] * hist_ref[CONV_HALO - lag:CONV_HALO - lag + c, :]
    hist_ref[0:CONV_HALO, :] = hist_ref[c:c + CONV_HALO, :]
    qkv = _silu(conv)

    ba = ba_ref[...]
    beta_all = 1.0 / (1.0 + jnp.exp(-ba))
    z = ba + dtb_ref[...]
    softplus = jnp.maximum(z, 0.0) + jnp.log(1.0 + jnp.exp(-jnp.abs(z)))
    g_all = -jnp.exp(alog_ref[...]) * softplus

    row = lax.broadcasted_iota(jnp.int32, (c, c), 0)
    col = lax.broadcasted_iota(jnp.int32, (c, c), 1)
    lower = row >= col
    strict = row > col
    xor = row ^ col
    tri = jnp.where(lower, 1.0, 0.0)
    gcum_all = jnp.dot(tri, g_all, precision=HIGHEST, preferred_element_type=F32)
    gcum_t = gcum_all.T
    glast_all = gcum_all[c - 1:c, :]
    eg_all = jnp.exp(gcum_all)
    ekd_all = jnp.exp(glast_all - gcum_all)
    elast_all = jnp.exp(glast_all)

    heads = range(A_HEADS)
    qs, ks, vs, betas, e_gs, decays = [], [], [], [], [], []
    for h in heads:
        gl = A_HEADS + h
        q = qkv[:, h * A_DK:(h + 1) * A_DK]
        k = qkv[:, A_HEADS * A_DK + h * A_DK:A_HEADS * A_DK + (h + 1) * A_DK]
        qs.append(q * lax.rsqrt(jnp.sum(q * q, axis=-1, keepdims=True) + EPS) * (A_DK ** -0.5))
        ks.append(k * lax.rsqrt(jnp.sum(k * k, axis=-1, keepdims=True) + EPS))
        vs.append(qkv[:, 2 * A_HEADS * A_DK + h * A_DV:2 * A_HEADS * A_DK + (h + 1) * A_DV])
        betas.append(beta_all[:, h:h + 1])
        e_gs.append(eg_all[:, gl:gl + 1])
        decays.append(jnp.exp(jnp.where(lower, gcum_all[:, gl:gl + 1] - gcum_t[gl:gl + 1, :], -jnp.inf)))
    k_betas = [k * b for k, b in zip(ks, betas)]
    a_mats = [jnp.where(strict, _mm_nt(kb, k) * d, 0.0) for kb, k, d in zip(k_betas, ks, decays)]
    qks = [_mm_nt(q, k) * d for q, k, d in zip(qs, ks, decays)]
    t_invs = _unit_lower_inverses(a_mats, xor)
    sols = [_mm(t, jnp.concatenate([v * b, kb * e], axis=-1))
            for t, v, b, kb, e in zip(t_invs, vs, betas, k_betas, e_gs)]
    states = [state_ref[h] for h in heads]
    wss = [_mm(jnp.concatenate([sol[:, A_DV:], q * e], axis=0), s)
           for sol, q, e, s in zip(sols, qs, e_gs, states)]
    v_news = [(sol[:, :A_DV] - ws[:c]).astype(BF16) for sol, ws in zip(sols, wss)]
    outs = [ws[c:] + _mm(qk, vn) for ws, qk, vn in zip(wss, qks, v_news)]
    for h in heads:
        gl = A_HEADS + h
        k_dec = ks[h] * ekd_all[:, gl:gl + 1]
        state_ref[h] = states[h] * elast_all[:, gl:gl + 1] + lax.dot_general(
            k_dec.astype(BF16), v_news[h], (((0,), (0,)), ((), ())), preferred_element_type=F32)
    for h in heads:
        gate = main_ref[:, A_QKV + h * A_DV:A_QKV + (h + 1) * A_DV]
        o_ref[:, h * A_DV:(h + 1) * A_DV] = (_rms(outs[h], onorm_ref[...]) * _silu(gate)).astype(BF16)


def _delta_mixer_core(main, ba, conv_w, a_log, dt_bias, o_norm, batch, seq):
    n_chunks = seq // CHUNK
    lane_pad = lambda p, lo: jnp.pad(p, (lo, LANES - lo - A_HEADS)).reshape(1, LANES)
    rows = lambda b, n: (b * n_chunks + n, 0)
    return pl.pallas_call(
        _delta_body,
        grid=(batch, n_chunks),
        in_specs=[pl.BlockSpec((CHUNK, A_MAIN), rows), pl.BlockSpec((CHUNK, LANES), rows),
                  _const_spec((CONV_K, A_QKV)), _const_spec((1, LANES)), _const_spec((1, LANES)),
                  _const_spec((1, A_DV))],
        out_specs=pl.BlockSpec((CHUNK, A_HEADS * A_DV), rows),
        out_shape=jax.ShapeDtypeStruct((batch * seq, A_HEADS * A_DV), BF16),
        scratch_shapes=[pltpu.VMEM((CHUNK + CONV_HALO, A_QKV), F32),
                        pltpu.VMEM((A_HEADS, A_DK, A_DV), F32)],
        compiler_params=_params("parallel", "arbitrary"),
        name="delta_rule",
    )(main, ba, conv_w, lane_pad(a_log, A_HEADS), lane_pad(dt_bias, A_HEADS), o_norm.reshape(1, A_DV))


def _kv_body(x_ref, g_ref, w_ref, b_ref, kv_ref):
    h = _rms(x_ref[...], g_ref[...]).astype(BF16)
    kv_ref[...] = (jnp.dot(h, w_ref[...], preferred_element_type=F32) + b_ref[...]).astype(BF16)


def _shared_kv(x, g, w, b):
    t = x.shape[0]
    n = w.shape[1]
    return pl.pallas_call(
        _kv_body,
        grid=(t // PROJ_ROWS,),
        in_specs=[pl.BlockSpec((PROJ_ROWS, D_MODEL), lambda i: (i, 0)), _const_spec((1, D_MODEL)),
                  _const_spec((D_MODEL, n)), _const_spec((1, n))],
        out_specs=pl.BlockSpec((PROJ_ROWS, n), lambda i: (i, 0)),
        out_shape=jax.ShapeDtypeStruct((t, n), BF16),
        compiler_params=_params("parallel"),
        name="shared_kv",
    )(x, g.reshape(1, D_MODEL), w.astype(BF16), b.reshape(1, n))


def _swa_body(sink_ref, x_ref, g_ref, wq_ref, bq_ref, kvp_ref, kvc_ref, wo_ref, y_ref, q_ref, o_ref):
    first_tile = pl.program_id(1) == 0
    kv_half = B_KV_HEADS * B_DH
    x = x_ref[...]
    h = _rms(x, g_ref[...]).astype(BF16)
    q_ref[...] = ((jnp.dot(h, wq_ref[...], preferred_element_type=F32) + bq_ref[...]) * (B_DH ** -0.5)).astype(BF16)

    qi = lax.broadcasted_iota(jnp.int32, (WINDOW, 2 * WINDOW), 0)
    ki = lax.broadcasted_iota(jnp.int32, (WINDOW, 2 * WINDOW), 1)
    in_win = (ki > qi) & (ki <= qi + WINDOW)
    for sub in range(SWA_ROWS // WINDOW):
        r0 = sub * WINDOW
        if sub == 0:
            keys = jnp.concatenate([kvp_ref[...], kvc_ref[0:WINDOW, :]], axis=0)
            mask = in_win & (jnp.logical_not(first_tile) | (ki >= WINDOW))
        else:
            keys = kvc_ref[r0 - WINDOW:r0 + WINDOW, :]
            mask = in_win
        for hk in range(B_KV_HEADS):
            kh = keys[:, hk * B_DH:(hk + 1) * B_DH]
            vh = keys[:, kv_half + hk * B_DH:kv_half + (hk + 1) * B_DH]
            for g in range(B_GROUP):
                hq = hk * B_GROUP + g
                sink = sink_ref[hq]
                s = lax.dot_general(q_ref[r0:r0 + WINDOW, hq * B_DH:(hq + 1) * B_DH], kh,
                                    (((1,), (1,)), ((), ())), preferred_element_type=F32)
                s = jnp.where(mask, s, -jnp.inf)
                m = jnp.maximum(jnp.max(s, axis=-1, keepdims=True), sink)
                p = jnp.exp(s - m)
                denom = jnp.sum(p, axis=-1, keepdims=True) + jnp.exp(sink - m)
                o = jnp.dot(p.astype(BF16), vh, preferred_element_type=F32) / denom
                o_ref[r0:r0 + WINDOW, hq * B_DH:(hq + 1) * B_DH] = o.astype(BF16)
    y_ref[...] = x + jnp.dot(o_ref[...], wo_ref[...], preferred_element_type=F32)


def _swa_mixer(x, g, w_q, b_q, sinks, w_out, kv, batch, seq):
    tiles = seq // SWA_ROWS
    per_tile = SWA_ROWS // WINDOW
    dq = B_Q_HEADS * B_DH
    row = pl.BlockSpec((SWA_ROWS, D_MODEL), lambda b, i: (b * tiles + i, 0))
    prev = pl.BlockSpec((WINDOW, kv.shape[1]),
                        lambda b, i: ((b * tiles + i) * per_tile - jnp.minimum(i, 1), 0))
    cur = pl.BlockSpec((SWA_ROWS, kv.shape[1]), lambda b, i: (b * tiles + i, 0))
    return pl.pallas_call(
        _swa_body,
        grid=(batch, tiles),
        in_specs=[pl.BlockSpec(memory_space=pltpu.SMEM), row, _const_spec((1, D_MODEL)),
                  _const_spec((D_MODEL, dq)), _const_spec((1, dq)), prev, cur, _const_spec((dq, D_MODEL))],
        out_specs=row,
        out_shape=jax.ShapeDtypeStruct((batch * seq, D_MODEL), F32),
        scratch_shapes=[pltpu.VMEM((SWA_ROWS, dq), BF16), pltpu.VMEM((SWA_ROWS, dq), BF16)],
        compiler_params=_params("parallel", "arbitrary"),
        name="swa_mixer",
    )(sinks, x, g.reshape(1, D_MODEL), w_q.astype(BF16), b_q.reshape(1, dq), kv, kv, w_out.astype(BF16))


def kernel(x, ffn_norm, ffn_w_gate, ffn_w_up, ffn_w_down, mix_norm, a_w_in, a_conv, a_A_log, a_dt_bias,
           a_o_norm, a_w_out, kv_norm, kv_w, kv_b, b_w_q, b_b_q, b_sinks, b_w_out, final_norm):
    batch, seq, d = x.shape
    x = x.reshape(batch * seq, d)
    kv = None
    for i in range(DEPTH):
        x = _ffn(x, ffn_norm[i, 0], ffn_w_gate[i, 0], ffn_w_up[i, 0], ffn_w_down[i, 0])
        if i < N_A_LAYERS:
            main, ba = _a_in_proj(x, mix_norm[i], a_w_in[i])
            o = _delta_mixer_core(main, ba, a_conv[i], a_A_log[i], a_dt_bias[i], a_o_norm[i], batch, seq)
            x = _out_proj(x, o, a_w_out[i])
        else:
            j = i - N_A_LAYERS
            x = _swa_mixer(x, mix_norm[i], b_w_q[j], b_b_q[j], b_sinks[j], b_w_out[j], kv, batch, seq)
        last = i == DEPTH - 1
        x = _ffn(x, ffn_norm[i, 1], ffn_w_gate[i, 1], ffn_w_up[i, 1], ffn_w_down[i, 1],
                 final_gain=final_norm if last else None)
        if i == N_A_LAYERS - 1:
            kv = _shared_kv(x, kv_norm, kv_w, kv_b)
    return x.reshape(batch, seq, d)
```

```python
import functools
import math

import jax
import jax.numpy as jnp
from jax import lax
from jax.experimental import pallas as pl
from jax.experimental.pallas import tpu as pltpu

F32 = jnp.float32
BF16 = jnp.bfloat16
HIGHEST = lax.Precision.HIGHEST

D_MODEL = 1024
D_FF = 2816
DEPTH = 4
N_A_LAYERS = 2
A_HEADS = 8
A_DK = 128
A_DV = 128
CONV_K = 4
CHUNK = 128
INV_BASE = 8
A_QKV = A_HEADS * (2 * A_DK + A_DV)
A_MAIN = A_QKV + A_HEADS * A_DV
B_Q_HEADS = 16
B_KV_HEADS = 4
B_GROUP = B_Q_HEADS // B_KV_HEADS
B_DH = 64
WINDOW = 128
EPS = 1e-6

LANES = 128
CONV_HALO = 8
VMEM_LIMIT = 56 * 1024 * 1024

FFN_ROWS = 512
PROJ_ROWS = 512
SWA_ROWS = 256
SWA_AHEAD = 2


def _rms(x, g):
    return x * lax.rsqrt(jnp.mean(x * x, axis=-1, keepdims=True) + EPS) * g


def _silu(x):
    return x / (1.0 + jnp.exp(-x))


def _const_spec(shape):
    return pl.BlockSpec(shape, lambda *_: (0,) * len(shape), pipeline_mode=pl.Buffered(1))


def _params(*semantics):
    return pltpu.CompilerParams(dimension_semantics=semantics, vmem_limit_bytes=VMEM_LIMIT)


def _ffn_body(x_ref, g_ref, wg_ref, wu_ref, wd_ref, fg_ref, o_ref, *, final):
    x = x_ref[...]
    h = _rms(x, g_ref[...]).astype(BF16)
    gate = jnp.dot(h, wg_ref[...], preferred_element_type=F32)
    up = jnp.dot(h, wu_ref[...], preferred_element_type=F32)
    act = (_silu(gate) * up).astype(BF16)
    out = x + 0.5 * jnp.dot(act, wd_ref[...], preferred_element_type=F32)
    if final:
        out = _rms(out, fg_ref[...])
    o_ref[...] = out


def _ffn(x, g, wg, wu, wd, final_gain=None):
    t = x.shape[0]
    final = final_gain is not None
    fg = final_gain if final else g
    row = pl.BlockSpec((FFN_ROWS, D_MODEL), lambda i: (i, 0))
    return pl.pallas_call(
        functools.partial(_ffn_body, final=final),
        grid=(t // FFN_ROWS,),
        in_specs=[row, _const_spec((1, D_MODEL)), _const_spec((D_MODEL, D_FF)), _const_spec((D_MODEL, D_FF)),
                  _const_spec((D_FF, D_MODEL)), _const_spec((1, D_MODEL))],
        out_specs=row,
        out_shape=jax.ShapeDtypeStruct((t, D_MODEL), F32),
        compiler_params=_params("parallel"),
        name="ffn",
    )(x, g.reshape(1, D_MODEL), wg.astype(BF16), wu.astype(BF16), wd.astype(BF16), fg.reshape(1, D_MODEL))


def _a_in_body(x_ref, g_ref, wm_ref, wb_ref, cw_ref, main_ref, ba_ref, hist_ref, *, tiles_per_seq):
    r = PROJ_ROWS

    @pl.when(pl.program_id(0) % tiles_per_seq == 0)
    def _():
        hist_ref[0:CONV_HALO, :] = jnp.zeros((CONV_HALO, A_QKV), F32)

    h = _rms(x_ref[...], g_ref[...]).astype(BF16)
    ba_ref[...] = jnp.dot(h, wb_ref[...], preferred_element_type=F32)
    main_ref[:, A_QKV:] = jnp.dot(h, wm_ref[:, A_QKV:], preferred_element_type=F32)
    hist_ref[CONV_HALO:CONV_HALO + r, :] = jnp.dot(h, wm_ref[:, :A_QKV], preferred_element_type=F32)

    for blk in range(A_QKV // LANES):
        lanes = slice(blk * LANES, (blk + 1) * LANES)
        conv = cw_ref[CONV_K - 1:CONV_K, lanes] * hist_ref[CONV_HALO:CONV_HALO + r, lanes]
        for j in range(CONV_K - 1):
            lag = CONV_K - 1 - j
            conv = conv + cw_ref[j:j + 1, lanes] * hist_ref[CONV_HALO - lag:CONV_HALO - lag + r, lanes]
        y = _silu(conv)
        if blk < 2 * A_HEADS:
            y = y * lax.rsqrt(jnp.sum(y * y, axis=-1, keepdims=True) + EPS)
        if blk < A_HEADS:
            y = y * (A_DK ** -0.5)
        main_ref[:, lanes] = y
    hist_ref[0:CONV_HALO, :] = hist_ref[r:r + CONV_HALO, :]


def _a_in_proj(x, g, w_in, conv_w, seq):
    t = x.shape[0]
    w_main = w_in[:, :A_MAIN].astype(BF16)
    w_ba = jnp.pad(w_in[:, A_MAIN:], ((0, 0), (0, LANES - 2 * A_HEADS))).astype(BF16)
    return pl.pallas_call(
        functools.partial(_a_in_body, tiles_per_seq=seq // PROJ_ROWS),
        grid=(t // PROJ_ROWS,),
        in_specs=[pl.BlockSpec((PROJ_ROWS, D_MODEL), lambda i: (i, 0)), _const_spec((1, D_MODEL)),
                  _const_spec((D_MODEL, A_MAIN)), _const_spec((D_MODEL, LANES)), _const_spec((CONV_K, A_QKV))],
        out_specs=[pl.BlockSpec((PROJ_ROWS, A_MAIN), lambda i: (i, 0)),
                   pl.BlockSpec((PROJ_ROWS, LANES), lambda i: (i, 0))],
        out_shape=[jax.ShapeDtypeStruct((t, A_MAIN), F32), jax.ShapeDtypeStruct((t, LANES), F32)],
        scratch_shapes=[pltpu.VMEM((PROJ_ROWS + CONV_HALO, A_QKV), F32)],
        compiler_params=_params("arbitrary"),
        name="a_in_proj",
    )(x, g.reshape(1, D_MODEL), w_main, w_ba, conv_w)


def _out_proj_body(x_ref, o_ref, w_ref, y_ref):
    y_ref[...] = x_ref[...] + jnp.dot(o_ref[...], w_ref[...], preferred_element_type=F32)


def _out_proj(x, o, w):
    t = x.shape[0]
    row = pl.BlockSpec((PROJ_ROWS, D_MODEL), lambda i: (i, 0))
    return pl.pallas_call(
        _out_proj_body,
        grid=(t // PROJ_ROWS,),
        in_specs=[row, pl.BlockSpec((PROJ_ROWS, w.shape[0]), lambda i: (i, 0)), _const_spec(w.shape)],
        out_specs=row,
        out_shape=jax.ShapeDtypeStruct((t, D_MODEL), F32),
        compiler_params=_params("parallel"),
        name="out_proj",
    )(x, o, w.astype(BF16))


def _mm(a, b):
    return jnp.dot(a.astype(BF16), b.astype(BF16), preferred_element_type=F32)


def _mm_nt(a, b):
    return lax.dot_general(a.astype(BF16), b.astype(BF16), (((1,), (1,)), ((), ())), preferred_element_type=F32)


def _unit_lower_inverses(mats, xor):
    c = mats[0].shape[0]
    ps = [jnp.where(xor < INV_BASE, -a, 0.0) for a in mats]
    ts = [jnp.where(xor == 0, 1.0, 0.0) + p for p in ps]
    k = 2
    while k < INV_BASE:
        ps = [_mm(p, p) for p in ps]
        ts = [t + _mm(t, p) for t, p in zip(ts, ps)]
        k *= 2
    s = INV_BASE
    while s < c:
        off = (xor >= s) & (xor < 2 * s)
        ats = [_mm(jnp.where(off, a, 0.0), t) for a, t in zip(mats, ts)]
        ts = [t - _mm(t, at) for t, at in zip(ts, ats)]
        s *= 2
    return ts


def _delta_body(main_ref, ba_ref, alog_ref, dtb_ref, onorm_ref, o_ref, state_ref):
    c = CHUNK

    @pl.when(pl.program_id(1) == 0)
    def _():
        state_ref[...] = jnp.zeros(state_ref.shape, F32)

    ba = ba_ref[...]
    beta_all = 1.0 / (1.0 + jnp.exp(-ba))
    z = ba + dtb_ref[...]
    softplus = jnp.maximum(z, 0.0) + jnp.log(1.0 + jnp.exp(-jnp.abs(z)))
    g_all = -jnp.exp(alog_ref[...]) * softplus

    row = lax.broadcasted_iota(jnp.int32, (c, c), 0)
    col = lax.broadcasted_iota(jnp.int32, (c, c), 1)
    lower = row >= col
    strict = row > col
    xor = row ^ col
    tri = jnp.where(lower, 1.0, 0.0)
    gcum_all = jnp.dot(tri, g_all, precision=HIGHEST, preferred_element_type=F32)
    gcum_t = gcum_all.T
    glast_all = gcum_all[c - 1:c, :]
    eg_all = jnp.exp(gcum_all)
    ekd_all = jnp.exp(glast_all - gcum_all)
    elast_all = jnp.exp(glast_all)

    heads = range(A_HEADS)
    qs, ks, vs, betas, e_gs, decays = [], [], [], [], [], []
    for h in heads:
        gl = A_HEADS + h
        qs.append(main_ref[:, h * A_DK:(h + 1) * A_DK])
        ks.append(main_ref[:, A_HEADS * A_DK + h * A_DK:A_HEADS * A_DK + (h + 1) * A_DK])
        vs.append(main_ref[:, 2 * A_HEADS * A_DK + h * A_DV:2 * A_HEADS * A_DK + (h + 1) * A_DV])
        betas.append(beta_all[:, h:h + 1])
        e_gs.append(eg_all[:, gl:gl + 1])
        decays.append(jnp.exp(jnp.where(lower, gcum_all[:, gl:gl + 1] - gcum_t[gl:gl + 1, :], -jnp.inf)))
    k_betas = [k * b for k, b in zip(ks, betas)]
    a_mats = [jnp.where(strict, _mm_nt(kb, k) * d, 0.0) for kb, k, d in zip(k_betas, ks, decays)]
    qks = [_mm_nt(q, k) * d for q, k, d in zip(qs, ks, decays)]
    t_invs = _unit_lower_inverses(a_mats, xor)
    sols = [_mm(t, jnp.concatenate([v * b, kb * e], axis=-1))
            for t, v, b, kb, e in zip(t_invs, vs, betas, k_betas, e_gs)]
    states = [state_ref[h] for h in heads]
    wss = [_mm(jnp.concatenate([sol[:, A_DV:], q * e], axis=0), s)
           for sol, q, e, s in zip(sols, qs, e_gs, states)]
    v_news = [(sol[:, :A_DV] - ws[:c]).astype(BF16) for sol, ws in zip(sols, wss)]
    outs = [ws[c:] + _mm(qk, vn) for ws, qk, vn in zip(wss, qks, v_news)]
    for h in heads:
        gl = A_HEADS + h
        k_dec = ks[h] * ekd_all[:, gl:gl + 1]
        state_ref[h] = states[h] * elast_all[:, gl:gl + 1] + lax.dot_general(
            k_dec.astype(BF16), v_news[h], (((0,), (0,)), ((), ())), preferred_element_type=F32)
    for h in heads:
        gate = main_ref[:, A_QKV + h * A_DV:A_QKV + (h + 1) * A_DV]
        o_ref[:, h * A_DV:(h + 1) * A_DV] = (_rms(outs[h], onorm_ref[...]) * _silu(gate)).astype(BF16)


def _delta_mixer_core(main, ba, a_log, dt_bias, o_norm, batch, seq):
    n_chunks = seq // CHUNK
    lane_pad = lambda p, lo: jnp.pad(p, (lo, LANES - lo - A_HEADS)).reshape(1, LANES)
    rows = lambda b, n: (b * n_chunks + n, 0)
    return pl.pallas_call(
        _delta_body,
        grid=(batch, n_chunks),
        in_specs=[pl.BlockSpec((CHUNK, A_MAIN), rows), pl.BlockSpec((CHUNK, LANES), rows),
                  _const_spec((1, LANES)), _const_spec((1, LANES)), _const_spec((1, A_DV))],
        out_specs=pl.BlockSpec((CHUNK, A_HEADS * A_DV), rows),
        out_shape=jax.ShapeDtypeStruct((batch * seq, A_HEADS * A_DV), BF16),
        scratch_shapes=[pltpu.VMEM((A_HEADS, A_DK, A_DV), F32)],
        compiler_params=_params("parallel", "arbitrary"),
        name="delta_rule",
    )(main, ba, lane_pad(a_log, A_HEADS), lane_pad(dt_bias, A_HEADS), o_norm.reshape(1, A_DV))


def _kv_body(x_ref, g_ref, w_ref, b_ref, kv_ref):
    h = _rms(x_ref[...], g_ref[...]).astype(BF16)
    kv_ref[...] = (jnp.dot(h, w_ref[...], preferred_element_type=F32) + b_ref[...]).astype(BF16)


def _shared_kv(x, g, w, b):
    t = x.shape[0]
    n = w.shape[1]
    return pl.pallas_call(
        _kv_body,
        grid=(t // PROJ_ROWS,),
        in_specs=[pl.BlockSpec((PROJ_ROWS, D_MODEL), lambda i: (i, 0)), _const_spec((1, D_MODEL)),
                  _const_spec((D_MODEL, n)), _const_spec((1, n))],
        out_specs=pl.BlockSpec((PROJ_ROWS, n), lambda i: (i, 0)),
        out_shape=jax.ShapeDtypeStruct((t, n), BF16),
        compiler_params=_params("parallel"),
        name="shared_kv",
    )(x, g.reshape(1, D_MODEL), w.astype(BF16), b.reshape(1, n))


def _swa_body(sink_ref, x_ref, g_ref, wq_ref, bq_ref, kvp_ref, kvc_ref, wo_ref, y_ref, q_ref, o_ref):
    first_tile = pl.program_id(1) == 0
    kv_half = B_KV_HEADS * B_DH
    x = x_ref[...]
    h = _rms(x, g_ref[...]).astype(BF16)
    q_ref[...] = ((jnp.dot(h, wq_ref[...], preferred_element_type=F32) + bq_ref[...]) * (B_DH ** -0.5)).astype(BF16)

    qi = lax.broadcasted_iota(jnp.int32, (WINDOW, 2 * WINDOW), 0)
    ki = lax.broadcasted_iota(jnp.int32, (WINDOW, 2 * WINDOW), 1)
    in_win = (ki > qi) & (ki <= qi + WINDOW)
    bias_inner = jnp.where(in_win, 0.0, -jnp.inf)
    bias_first = jnp.where(in_win & (jnp.logical_not(first_tile) | (ki >= WINDOW)), 0.0, -jnp.inf)
    items = [(sub, hq) for sub in range(SWA_ROWS // WINDOW) for hq in range(B_Q_HEADS)]
    keys = [jnp.concatenate([kvp_ref[...], kvc_ref[0:WINDOW, :]], axis=0)]
    keys += [kvc_ref[(sub - 1) * WINDOW:(sub + 1) * WINDOW, :] for sub in range(1, SWA_ROWS // WINDOW)]

    def scores_of(item):
        sub, hq = item
        hk = hq // B_GROUP
        s = lax.dot_general(q_ref[sub * WINDOW:(sub + 1) * WINDOW, hq * B_DH:(hq + 1) * B_DH],
                            keys[sub][:, hk * B_DH:(hk + 1) * B_DH], (((1,), (1,)), ((), ())),
                            preferred_element_type=F32)
        return s + (bias_first if sub == 0 else bias_inner)

    def finish(item, s):
        sub, hq = item
        hk = hq // B_GROUP
        sink = sink_ref[hq]
        m = jnp.maximum(jnp.max(s, axis=-1, keepdims=True), sink)
        p = jnp.exp(s - m)
        denom = jnp.sum(p, axis=-1, keepdims=True) + jnp.exp(sink - m)
        o = jnp.dot(p.astype(BF16), keys[sub][:, kv_half + hk * B_DH:kv_half + (hk + 1) * B_DH],
                    preferred_element_type=F32) / denom
        o_ref[sub * WINDOW:(sub + 1) * WINDOW, hq * B_DH:(hq + 1) * B_DH] = o.astype(BF16)

    pending = [scores_of(it) for it in items[:SWA_AHEAD]]
    for i, it in enumerate(items):
        if i + SWA_AHEAD < len(items):
            pending.append(scores_of(items[i + SWA_AHEAD]))
        finish(it, pending.pop(0))
    y_ref[...] = x + jnp.dot(o_ref[...], wo_ref[...], preferred_element_type=F32)


def _swa_mixer(x, g, w_q, b_q, sinks, w_out, kv, batch, seq):
    tiles = seq // SWA_ROWS
    per_tile = SWA_ROWS // WINDOW
    dq = B_Q_HEADS * B_DH
    row = pl.BlockSpec((SWA_ROWS, D_MODEL), lambda b, i: (b * tiles + i, 0))
    prev = pl.BlockSpec((WINDOW, kv.shape[1]),
                        lambda b, i: ((b * tiles + i) * per_tile - jnp.minimum(i, 1), 0))
    cur = pl.BlockSpec((SWA_ROWS, kv.shape[1]), lambda b, i: (b * tiles + i, 0))
    return pl.pallas_call(
        _swa_body,
        grid=(batch, tiles),
        in_specs=[pl.BlockSpec(memory_space=pltpu.SMEM), row, _const_spec((1, D_MODEL)),
                  _const_spec((D_MODEL, dq)), _const_spec((1, dq)), prev, cur, _const_spec((dq, D_MODEL))],
        out_specs=row,
        out_shape=jax.ShapeDtypeStruct((batch * seq, D_MODEL), F32),
        scratch_shapes=[pltpu.VMEM((SWA_ROWS, dq), BF16), pltpu.VMEM((SWA_ROWS, dq), BF16)],
        compiler_params=_params("parallel", "arbitrary"),
        name="swa_mixer",
    )(sinks, x, g.reshape(1, D_MODEL), w_q.astype(BF16), b_q.reshape(1, dq), kv, kv, w_out.astype(BF16))


def kernel(x, ffn_norm, ffn_w_gate, ffn_w_up, ffn_w_down, mix_norm, a_w_in, a_conv, a_A_log, a_dt_bias,
           a_o_norm, a_w_out, kv_norm, kv_w, kv_b, b_w_q, b_b_q, b_sinks, b_w_out, final_norm):
    batch, seq, d = x.shape
    x = x.reshape(batch * seq, d)
    kv = None
    for i in range(DEPTH):
        x = _ffn(x, ffn_norm[i, 0], ffn_w_gate[i, 0], ffn_w_up[i, 0], ffn_w_down[i, 0])
        if i < N_A_LAYERS:
            main, ba = _a_in_proj(x, mix_norm[i], a_w_in[i], a_conv[i], seq)
            o = _delta_mixer_core(main, ba, a_A_log[i], a_dt_bias[i], a_o_norm[i], batch, seq)
            x = _out_proj(x, o, a_w_out[i])
        else:
            j = i - N_A_LAYERS
            x = _swa_mixer(x, mix_norm[i], b_w_q[j], b_b_q[j], b_sinks[j], b_w_out[j], kv, batch, seq)
        last = i == DEPTH - 1
        x = _ffn(x, ffn_norm[i, 1], ffn_w_gate[i, 1], ffn_w_up[i, 1], ffn_w_down[i, 1],
                 final_gain=final_norm if last else None)
        if i == N_A_LAYERS - 1:
            kv = _shared_kv(x, kv_norm, kv_w, kv_b)
    return x.reshape(batch, seq, d)
```

```python
import functools

import jax
import jax.numpy as jnp
from jax import lax
from jax.experimental import pallas as pl
from jax.experimental.pallas import tpu as pltpu

F32 = jnp.float32
BF16 = jnp.bfloat16
HIGHEST = lax.Precision.HIGHEST

D_MODEL = 1024
D_FF = 2816
DEPTH = 4
N_A_LAYERS = 2
A_HEADS = 8
A_DK = 128
A_DV = 128
CONV_K = 4
CHUNK = 128
INV_BASE = 8
A_QKV = A_HEADS * (2 * A_DK + A_DV)
A_MAIN = A_QKV + A_HEADS * A_DV
B_Q_HEADS = 16
B_KV_HEADS = 4
B_GROUP = B_Q_HEADS // B_KV_HEADS
B_DH = 64
WINDOW = 128
EPS = 1e-6

LANES = 128
CONV_HALO = 8
VMEM_LIMIT = 56 * 1024 * 1024

FFN_ROWS = 512
FFN_CHUNK = 256
PROJ_ROWS = 512
SWA_ROWS = 256
SWA_AHEAD = 2


def _rms(x, g):
    return x * lax.rsqrt(jnp.mean(x * x, axis=-1, keepdims=True) + EPS) * g


def _silu(x):
    return x / (1.0 + jnp.exp(-x))


def _const_spec(shape):
    return pl.BlockSpec(shape, lambda *_: (0,) * len(shape), pipeline_mode=pl.Buffered(1))


def _params(*semantics):
    return pltpu.CompilerParams(dimension_semantics=semantics, vmem_limit_bytes=VMEM_LIMIT)


def _swiglu_residual(x, g_ref, wg_ref, wu_ref, wd_ref):
    h = _rms(x, g_ref[...]).astype(BF16)
    acc = x
    for c0 in range(0, D_FF, FFN_CHUNK):
        cols = slice(c0, c0 + FFN_CHUNK)
        gate = jnp.dot(h, wg_ref[:, cols], preferred_element_type=F32)
        up = jnp.dot(h, wu_ref[:, cols], preferred_element_type=F32)
        act = (_silu(gate) * (0.5 * up)).astype(BF16)
        acc = acc + jnp.dot(act, wd_ref[cols, :], preferred_element_type=F32)
    return acc


def _ffn_body(*refs, pre, post):
    it = iter(refs)
    x_ref = next(it)
    if pre:
        o_ref, wo_ref = next(it), next(it)
    g_ref, wg_ref, wu_ref, wd_ref = next(it), next(it), next(it), next(it)
    if post == "final":
        fg_ref = next(it)
    elif post == "kv":
        kg_ref, kw_ref, kb_ref = next(it), next(it), next(it)
    y_ref = next(it)

    x = x_ref[...]
    if pre:
        x = x + jnp.dot(o_ref[...], wo_ref[...], preferred_element_type=F32)
    y = _swiglu_residual(x, g_ref, wg_ref, wu_ref, wd_ref)
    if post == "final":
        y = _rms(y, fg_ref[...])
    y_ref[...] = y
    if post == "kv":
        kv_ref = next(it)
        hk = _rms(y, kg_ref[...]).astype(BF16)
        kv_ref[...] = (jnp.dot(hk, kw_ref[...], preferred_element_type=F32) + kb_ref[...]).astype(BF16)


def _ffn(x, g, wg, wu, wd, pre=None, post=None, post_args=()):
    t = x.shape[0]
    row = pl.BlockSpec((FFN_ROWS, D_MODEL), lambda i: (i, 0))
    vec = lambda v: v.reshape(1, v.shape[-1])
    args, specs = [x], [row]
    if pre is not None:
        o, w_out = pre
        args += [o, w_out.astype(BF16)]
        specs += [pl.BlockSpec((FFN_ROWS, o.shape[1]), lambda i: (i, 0)), _const_spec(w_out.shape)]
    args += [vec(g), wg.astype(BF16), wu.astype(BF16), wd.astype(BF16)]
    specs += [_const_spec((1, D_MODEL)), _const_spec((D_MODEL, D_FF)), _const_spec((D_MODEL, D_FF)),
              _const_spec((D_FF, D_MODEL))]
    out_shape = [jax.ShapeDtypeStruct((t, D_MODEL), F32)]
    out_specs = [row]
    if post == "final":
        args += [vec(post_args[0])]
        specs += [_const_spec((1, D_MODEL))]
    elif post == "kv":
        kg, kw, kb = post_args
        n = kw.shape[1]
        args += [vec(kg), kw.astype(BF16), vec(kb)]
        specs += [_const_spec((1, D_MODEL)), _const_spec((D_MODEL, n)), _const_spec((1, n))]
        out_shape.append(jax.ShapeDtypeStruct((t, n), BF16))
        out_specs.append(pl.BlockSpec((FFN_ROWS, n), lambda i: (i, 0)))
    outs = pl.pallas_call(
        functools.partial(_ffn_body, pre=pre is not None, post=post),
        grid=(t // FFN_ROWS,),
        in_specs=specs,
        out_specs=out_specs,
        out_shape=out_shape,
        compiler_params=_params("parallel"),
        name="ffn" + ("_pre" if pre is not None else "") + ("_" + post if post else ""),
    )(*args)
    return outs[0] if len(outs) == 1 else outs


def _a_in_body(x_ref, g_ref, wm_ref, wb_ref, cw_ref, main_ref, ba_ref, hist_ref, *, tiles_per_seq):
    r = PROJ_ROWS

    @pl.when(pl.program_id(0) % tiles_per_seq == 0)
    def _():
        hist_ref[0:CONV_HALO, :] = jnp.zeros((CONV_HALO, A_QKV), F32)

    h = _rms(x_ref[...], g_ref[...]).astype(BF16)
    ba_ref[...] = jnp.dot(h, wb_ref[...], preferred_element_type=F32)
    main_ref[:, A_QKV:] = jnp.dot(h, wm_ref[:, A_QKV:], preferred_element_type=F32)
    hist_ref[CONV_HALO:CONV_HALO + r, :] = jnp.dot(h, wm_ref[:, :A_QKV], preferred_element_type=F32)

    for blk in range(A_QKV // LANES):
        lanes = slice(blk * LANES, (blk + 1) * LANES)
        conv = cw_ref[CONV_K - 1:CONV_K, lanes] * hist_ref[CONV_HALO:CONV_HALO + r, lanes]
        for j in range(CONV_K - 1):
            lag = CONV_K - 1 - j
            conv = conv + cw_ref[j:j + 1, lanes] * hist_ref[CONV_HALO - lag:CONV_HALO - lag + r, lanes]
        y = _silu(conv)
        if blk < 2 * A_HEADS:
            y = y * lax.rsqrt(jnp.sum(y * y, axis=-1, keepdims=True) + EPS)
        if blk < A_HEADS:
            y = y * (A_DK ** -0.5)
        main_ref[:, lanes] = y
    hist_ref[0:CONV_HALO, :] = hist_ref[r:r + CONV_HALO, :]


def _a_in_proj(x, g, w_in, conv_w, seq):
    t = x.shape[0]
    w_main = w_in[:, :A_MAIN].astype(BF16)
    w_ba = jnp.pad(w_in[:, A_MAIN:], ((0, 0), (0, LANES - 2 * A_HEADS))).astype(BF16)
    return pl.pallas_call(
        functools.partial(_a_in_body, tiles_per_seq=seq // PROJ_ROWS),
        grid=(t // PROJ_ROWS,),
        in_specs=[pl.BlockSpec((PROJ_ROWS, D_MODEL), lambda i: (i, 0)), _const_spec((1, D_MODEL)),
                  _const_spec((D_MODEL, A_MAIN)), _const_spec((D_MODEL, LANES)), _const_spec((CONV_K, A_QKV))],
        out_specs=[pl.BlockSpec((PROJ_ROWS, A_MAIN), lambda i: (i, 0)),
                   pl.BlockSpec((PROJ_ROWS, LANES), lambda i: (i, 0))],
        out_shape=[jax.ShapeDtypeStruct((t, A_MAIN), F32), jax.ShapeDtypeStruct((t, LANES), F32)],
        scratch_shapes=[pltpu.VMEM((PROJ_ROWS + CONV_HALO, A_QKV), F32)],
        compiler_params=_params("arbitrary"),
        name="a_in_proj",
    )(x, g.reshape(1, D_MODEL), w_main, w_ba, conv_w)


def _mm(a, b):
    return jnp.dot(a.astype(BF16), b.astype(BF16), preferred_element_type=F32)


def _mm_nt(a, b):
    return lax.dot_general(a.astype(BF16), b.astype(BF16), (((1,), (1,)), ((), ())), preferred_element_type=F32)


def _unit_lower_inverses(mats, xor):
    c = mats[0].shape[0]
    ps = [jnp.where(xor < INV_BASE, -a, 0.0) for a in mats]
    ts = [jnp.where(xor == 0, 1.0, 0.0) + p for p in ps]
    k = 2
    while k < INV_BASE:
        ps = [_mm(p, p) for p in ps]
        ts = [t + _mm(t, p) for t, p in zip(ts, ps)]
        k *= 2
    s = INV_BASE
    while s < c:
        off = (xor >= s) & (xor < 2 * s)
        ats = [_mm(jnp.where(off, a, 0.0), t) for a, t in zip(mats, ts)]
        ts = [t - _mm(t, at) for t, at in zip(ts, ats)]
        s *= 2
    return ts


def _delta_body(main_ref, ba_ref, alog_ref, dtb_ref, onorm_ref, o_ref, state_ref):
    c = CHUNK

    @pl.when(pl.program_id(1) == 0)
    def _():
        state_ref[...] = jnp.zeros(state_ref.shape, F32)

    ba = ba_ref[...]
    beta_all = 1.0 / (1.0 + jnp.exp(-ba))
    z = ba + dtb_ref[...]
    softplus = jnp.maximum(z, 0.0) + jnp.log(1.0 + jnp.exp(-jnp.abs(z)))
    g_all = -jnp.exp(alog_ref[...]) * softplus

    row = lax.broadcasted_iota(jnp.int32, (c, c), 0)
    col = lax.broadcasted_iota(jnp.int32, (c, c), 1)
    lower = row >= col
    strict = row > col
    xor = row ^ col
    tri = jnp.where(lower, 1.0, 0.0)
    gcum_all = jnp.dot(tri, g_all, precision=HIGHEST, preferred_element_type=F32)
    gcum_t = gcum_all.T
    glast_all = gcum_all[c - 1:c, :]
    eg_all = jnp.exp(gcum_all)
    ekd_all = jnp.exp(glast_all - gcum_all)
    elast_all = jnp.exp(glast_all)

    heads = range(A_HEADS)
    qs, ks, vs, betas, e_gs, decays = [], [], [], [], [], []
    for h in heads:
        gl = A_HEADS + h
        qs.append(main_ref[:, h * A_DK:(h + 1) * A_DK])
        ks.append(main_ref[:, A_HEADS * A_DK + h * A_DK:A_HEADS * A_DK + (h + 1) * A_DK])
        vs.append(main_ref[:, 2 * A_HEADS * A_DK + h * A_DV:2 * A_HEADS * A_DK + (h + 1) * A_DV])
        betas.append(beta_all[:, h:h + 1])
        e_gs.append(eg_all[:, gl:gl + 1])
        decays.append(jnp.exp(jnp.where(lower, gcum_all[:, gl:gl + 1] - gcum_t[gl:gl + 1, :], -jnp.inf)))
    k_betas = [k * b for k, b in zip(ks, betas)]
    a_mats = [jnp.where(strict, _mm_nt(kb, k) * d, 0.0) for kb, k, d in zip(k_betas, ks, decays)]
    qks = [_mm_nt(q, k) * d for q, k, d in zip(qs, ks, decays)]
    t_invs = _unit_lower_inverses(a_mats, xor)
    sols = [_mm(t, jnp.concatenate([v * b, kb * e], axis=-1))
            for t, v, b, kb, e in zip(t_invs, vs, betas, k_betas, e_gs)]
    states = [state_ref[h] for h in heads]
    wss = [_mm(jnp.concatenate([sol[:, A_DV:], q * e], axis=0), s)
           for sol, q, e, s in zip(sols, qs, e_gs, states)]
    v_news = [(sol[:, :A_DV] - ws[:c]).astype(BF16) for sol, ws in zip(sols, wss)]
    outs = [ws[c:] + _mm(qk, vn) for ws, qk, vn in zip(wss, qks, v_news)]
    for h in heads:
        gl = A_HEADS + h
        k_dec = ks[h] * ekd_all[:, gl:gl + 1]
        state_ref[h] = states[h] * elast_all[:, gl:gl + 1] + lax.dot_general(
            k_dec.astype(BF16), v_news[h], (((0,), (0,)), ((), ())), preferred_element_type=F32)
    for h in heads:
        gate = main_ref[:, A_QKV + h * A_DV:A_QKV + (h + 1) * A_DV]
        o_ref[:, h * A_DV:(h + 1) * A_DV] = (_rms(outs[h], onorm_ref[...]) * _silu(gate)).astype(BF16)


def _delta_mixer_core(main, ba, a_log, dt_bias, o_norm, batch, seq):
    n_chunks = seq // CHUNK
    lane_pad = lambda p, lo: jnp.pad(p, (lo, LANES - lo - A_HEADS)).reshape(1, LANES)
    rows = lambda b, n: (b * n_chunks + n, 0)
    return pl.pallas_call(
        _delta_body,
        grid=(batch, n_chunks),
        in_specs=[pl.BlockSpec((CHUNK, A_MAIN), rows), pl.BlockSpec((CHUNK, LANES), rows),
                  _const_spec((1, LANES)), _const_spec((1, LANES)), _const_spec((1, A_DV))],
        out_specs=pl.BlockSpec((CHUNK, A_HEADS * A_DV), rows),
        out_shape=jax.ShapeDtypeStruct((batch * seq, A_HEADS * A_DV), BF16),
        scratch_shapes=[pltpu.VMEM((A_HEADS, A_DK, A_DV), F32)],
        compiler_params=_params("parallel", "arbitrary"),
        name="delta_rule",
    )(main, ba, lane_pad(a_log, A_HEADS), lane_pad(dt_bias, A_HEADS), o_norm.reshape(1, A_DV))


def _swa_body(sink_ref, x_ref, g_ref, wq_ref, bq_ref, kvp_ref, kvc_ref, wo_ref, y_ref, q_ref, o_ref):
    first_tile = pl.program_id(1) == 0
    kv_half = B_KV_HEADS * B_DH
    x = x_ref[...]
    h = _rms(x, g_ref[...]).astype(BF16)
    q_ref[...] = ((jnp.dot(h, wq_ref[...], preferred_element_type=F32) + bq_ref[...]) * (B_DH ** -0.5)).astype(BF16)

    qi = lax.broadcasted_iota(jnp.int32, (WINDOW, 2 * WINDOW), 0)
    ki = lax.broadcasted_iota(jnp.int32, (WINDOW, 2 * WINDOW), 1)
    in_win = (ki > qi) & (ki <= qi + WINDOW)
    bias_inner = jnp.where(in_win, 0.0, -jnp.inf)
    bias_first = jnp.where(in_win & (jnp.logical_not(first_tile) | (ki >= WINDOW)), 0.0, -jnp.inf)
    items = [(sub, hq) for sub in range(SWA_ROWS // WINDOW) for hq in range(B_Q_HEADS)]
    keys = [jnp.concatenate([kvp_ref[...], kvc_ref[0:WINDOW, :]], axis=0)]
    keys += [kvc_ref[(sub - 1) * WINDOW:(sub + 1) * WINDOW, :] for sub in range(1, SWA_ROWS // WINDOW)]

    def scores_of(item):
        sub, hq = item
        hk = hq // B_GROUP
        s = lax.dot_general(q_ref[sub * WINDOW:(sub + 1) * WINDOW, hq * B_DH:(hq + 1) * B_DH],
                            keys[sub][:, hk * B_DH:(hk + 1) * B_DH], (((1,), (1,)), ((), ())),
                            preferred_element_type=F32)
        return s + (bias_first if sub == 0 else bias_inner)

    def finish(item, s):
        sub, hq = item
        hk = hq // B_GROUP
        sink = sink_ref[hq]
        m = jnp.maximum(jnp.max(s, axis=-1, keepdims=True), sink)
        p = jnp.exp(s - m)
        denom = jnp.sum(p, axis=-1, keepdims=True) + jnp.exp(sink - m)
        o = jnp.dot(p.astype(BF16), keys[sub][:, kv_half + hk * B_DH:kv_half + (hk + 1) * B_DH],
                    preferred_element_type=F32) / denom
        o_ref[sub * WINDOW:(sub + 1) * WINDOW, hq * B_DH:(hq + 1) * B_DH] = o.astype(BF16)

    pending = [scores_of(it) for it in items[:SWA_AHEAD]]
    for i, it in enumerate(items):
        if i + SWA_AHEAD < len(items):
            pending.append(scores_of(items[i + SWA_AHEAD]))
        finish(it, pending.pop(0))
    y_ref[...] = x + jnp.dot(o_ref[...], wo_ref[...], preferred_element_type=F32)


def _swa_mixer(x, g, w_q, b_q, sinks, w_out, kv, batch, seq):
    tiles = seq // SWA_ROWS
    per_tile = SWA_ROWS // WINDOW
    dq = B_Q_HEADS * B_DH
    row = pl.BlockSpec((SWA_ROWS, D_MODEL), lambda b, i: (b * tiles + i, 0))
    prev = pl.BlockSpec((WINDOW, kv.shape[1]),
                        lambda b, i: ((b * tiles + i) * per_tile - jnp.minimum(i, 1), 0))
    cur = pl.BlockSpec((SWA_ROWS, kv.shape[1]), lambda b, i: (b * tiles + i, 0))
    return pl.pallas_call(
        _swa_body,
        grid=(batch, tiles),
        in_specs=[pl.BlockSpec(memory_space=pltpu.SMEM), row, _const_spec((1, D_MODEL)),
                  _const_spec((D_MODEL, dq)), _const_spec((1, dq)), prev, cur, _const_spec((dq, D_MODEL))],
        out_specs=row,
        out_shape=jax.ShapeDtypeStruct((batch * seq, D_MODEL), F32),
        scratch_shapes=[pltpu.VMEM((SWA_ROWS, dq), BF16), pltpu.VMEM((SWA_ROWS, dq), BF16)],
        compiler_params=_params("parallel", "arbitrary"),
        name="swa_mixer",
    )(sinks, x, g.reshape(1, D_MODEL), w_q.astype(BF16), b_q.reshape(1, dq), kv, kv, w_out.astype(BF16))


def kernel(x, ffn_norm, ffn_w_gate, ffn_w_up, ffn_w_down, mix_norm, a_w_in, a_conv, a_A_log, a_dt_bias,
           a_o_norm, a_w_out, kv_norm, kv_w, kv_b, b_w_q, b_b_q, b_sinks, b_w_out, final_norm):
    batch, seq, d = x.shape
    x = x.reshape(batch * seq, d)
    ffn_w = lambda i, s: (ffn_norm[i, s], ffn_w_gate[i, s], ffn_w_up[i, s], ffn_w_down[i, s])
    kv = None
    for i in range(DEPTH):
        x = _ffn(x, *ffn_w(i, 0))
        if i < N_A_LAYERS:
            main, ba = _a_in_proj(x, mix_norm[i], a_w_in[i], a_conv[i], seq)
            o = _delta_mixer_core(main, ba, a_A_log[i], a_dt_bias[i], a_o_norm[i], batch, seq)
            if i == N_A_LAYERS - 1:
                x, kv = _ffn(x, *ffn_w(i, 1), pre=(o, a_w_out[i]), post="kv", post_args=(kv_norm, kv_w, kv_b))
            else:
                x = _ffn(x, *ffn_w(i, 1), pre=(o, a_w_out[i]))
        else:
            j = i - N_A_LAYERS
            x = _swa_mixer(x, mix_norm[i], b_w_q[j], b_b_q[j], b_sinks[j], b_w_out[j], kv, batch, seq)
            if i == DEPTH - 1:
                x = _ffn(x, *ffn_w(i, 1), post="final", post_args=(final_norm,))
            else:
                x = _ffn(x, *ffn_w(i, 1))
    return x.reshape(batch, seq, d)
```

```python
import functools

import jax
import jax.numpy as jnp
from jax import lax
from jax.experimental import pallas as pl
from jax.experimental.pallas import tpu as pltpu

F32 = jnp.float32
BF16 = jnp.bfloat16
HIGHEST = lax.Precision.HIGHEST

D_MODEL = 1024
D_FF = 2816
DEPTH = 4
N_A_LAYERS = 2
A_HEADS = 8
A_DK = 128
A_DV = 128
CONV_K = 4
CHUNK = 128
INV_BASE = 8
A_QKV = A_HEADS * (2 * A_DK + A_DV)
A_MAIN = A_QKV + A_HEADS * A_DV
B_Q_HEADS = 16
B_KV_HEADS = 4
B_GROUP = B_Q_HEADS // B_KV_HEADS
B_DH = 64
WINDOW = 128
EPS = 1e-6

LANES = 128
CONV_HALO = 8
VMEM_LIMIT = 56 * 1024 * 1024

FFN_ROWS = 512
FFN_CHUNK = 256
PROJ_ROWS = 512
SWA_ROWS = 256
SWA_GROUP = 8


def _rms(x, g):
    return x * lax.rsqrt(jnp.mean(x * x, axis=-1, keepdims=True) + EPS) * g


def _silu(x):
    return x / (1.0 + jnp.exp(-x))


def _const_spec(shape):
    return pl.BlockSpec(shape, lambda *_: (0,) * len(shape), pipeline_mode=pl.Buffered(1))


def _params(*semantics):
    return pltpu.CompilerParams(dimension_semantics=semantics, vmem_limit_bytes=VMEM_LIMIT)


def _swiglu_residual(x, g_ref, wg_ref, wu_ref, wd_ref):
    h = _rms(x, g_ref[...]).astype(BF16)
    acc = x
    for c0 in range(0, D_FF, FFN_CHUNK):
        cols = slice(c0, c0 + FFN_CHUNK)
        gate = jnp.dot(h, wg_ref[:, cols], preferred_element_type=F32)
        up = jnp.dot(h, wu_ref[:, cols], preferred_element_type=F32)
        act = (_silu(gate) * (0.5 * up)).astype(BF16)
        acc = acc + jnp.dot(act, wd_ref[cols, :], preferred_element_type=F32)
    return acc


def _ffn_body(*refs, pre, post):
    it = iter(refs)
    x_ref = next(it)
    if pre:
        o_ref, wo_ref = next(it), next(it)
    g_ref, wg_ref, wu_ref, wd_ref = next(it), next(it), next(it), next(it)
    if post == "final":
        fg_ref = next(it)
    elif post == "kv":
        kg_ref, kw_ref, kb_ref = next(it), next(it), next(it)
    y_ref = next(it)

    x = x_ref[...]
    if pre:
        x = x + jnp.dot(o_ref[...], wo_ref[...], preferred_element_type=F32)
    y = _swiglu_residual(x, g_ref, wg_ref, wu_ref, wd_ref)
    if post == "final":
        y = _rms(y, fg_ref[...])
    y_ref[...] = y
    if post == "kv":
        kv_ref = next(it)
        hk = _rms(y, kg_ref[...]).astype(BF16)
        kv_ref[...] = (jnp.dot(hk, kw_ref[...], preferred_element_type=F32) + kb_ref[...]).astype(BF16)


def _ffn(x, g, wg, wu, wd, pre=None, post=None, post_args=()):
    t = x.shape[0]
    row = pl.BlockSpec((FFN_ROWS, D_MODEL), lambda i: (i, 0))
    vec = lambda v: v.reshape(1, v.shape[-1])
    args, specs = [x], [row]
    if pre is not None:
        o, w_out = pre
        args += [o, w_out.astype(BF16)]
        specs += [pl.BlockSpec((FFN_ROWS, o.shape[1]), lambda i: (i, 0)), _const_spec(w_out.shape)]
    args += [vec(g), wg.astype(BF16), wu.astype(BF16), wd.astype(BF16)]
    specs += [_const_spec((1, D_MODEL)), _const_spec((D_MODEL, D_FF)), _const_spec((D_MODEL, D_FF)),
              _const_spec((D_FF, D_MODEL))]
    out_shape = [jax.ShapeDtypeStruct((t, D_MODEL), F32)]
    out_specs = [row]
    if post == "final":
        args += [vec(post_args[0])]
        specs += [_const_spec((1, D_MODEL))]
    elif post == "kv":
        kg, kw, kb = post_args
        n = kw.shape[1]
        args += [vec(kg), kw.astype(BF16), vec(kb)]
        specs += [_const_spec((1, D_MODEL)), _const_spec((D_MODEL, n)), _const_spec((1, n))]
        out_shape.append(jax.ShapeDtypeStruct((t, n), BF16))
        out_specs.append(pl.BlockSpec((FFN_ROWS, n), lambda i: (i, 0)))
    outs = pl.pallas_call(
        functools.partial(_ffn_body, pre=pre is not None, post=post),
        grid=(t // FFN_ROWS,),
        in_specs=specs,
        out_specs=out_specs,
        out_shape=out_shape,
        compiler_params=_params("parallel"),
        name="ffn" + ("_pre" if pre is not None else "") + ("_" + post if post else ""),
    )(*args)
    return outs[0] if len(outs) == 1 else outs


def _a_in_body(x_ref, g_ref, wm_ref, wb_ref, cw_ref, main_ref, ba_ref, hist_ref, *, tiles_per_seq):
    r = PROJ_ROWS

    @pl.when(pl.program_id(0) % tiles_per_seq == 0)
    def _():
        hist_ref[0:CONV_HALO, :] = jnp.zeros((CONV_HALO, A_QKV), F32)

    h = _rms(x_ref[...], g_ref[...]).astype(BF16)
    ba_ref[...] = jnp.dot(h, wb_ref[...], preferred_element_type=F32)
    main_ref[:, A_QKV:] = jnp.dot(h, wm_ref[:, A_QKV:], preferred_element_type=F32)
    hist_ref[CONV_HALO:CONV_HALO + r, :] = jnp.dot(h, wm_ref[:, :A_QKV], preferred_element_type=F32)

    for blk in range(A_QKV // LANES):
        lanes = slice(blk * LANES, (blk + 1) * LANES)
        conv = cw_ref[CONV_K - 1:CONV_K, lanes] * hist_ref[CONV_HALO:CONV_HALO + r, lanes]
        for j in range(CONV_K - 1):
            lag = CONV_K - 1 - j
            conv = conv + cw_ref[j:j + 1, lanes] * hist_ref[CONV_HALO - lag:CONV_HALO - lag + r, lanes]
        y = _silu(conv)
        if blk < 2 * A_HEADS:
            y = y * lax.rsqrt(jnp.sum(y * y, axis=-1, keepdims=True) + EPS)
        if blk < A_HEADS:
            y = y * (A_DK ** -0.5)
        main_ref[:, lanes] = y
    hist_ref[0:CONV_HALO, :] = hist_ref[r:r + CONV_HALO, :]


def _a_in_proj(x, g, w_in, conv_w, seq):
    t = x.shape[0]
    w_main = w_in[:, :A_MAIN].astype(BF16)
    w_ba = jnp.pad(w_in[:, A_MAIN:], ((0, 0), (0, LANES - 2 * A_HEADS))).astype(BF16)
    return pl.pallas_call(
        functools.partial(_a_in_body, tiles_per_seq=seq // PROJ_ROWS),
        grid=(t // PROJ_ROWS,),
        in_specs=[pl.BlockSpec((PROJ_ROWS, D_MODEL), lambda i: (i, 0)), _const_spec((1, D_MODEL)),
                  _const_spec((D_MODEL, A_MAIN)), _const_spec((D_MODEL, LANES)), _const_spec((CONV_K, A_QKV))],
        out_specs=[pl.BlockSpec((PROJ_ROWS, A_MAIN), lambda i: (i, 0)),
                   pl.BlockSpec((PROJ_ROWS, LANES), lambda i: (i, 0))],
        out_shape=[jax.ShapeDtypeStruct((t, A_MAIN), F32), jax.ShapeDtypeStruct((t, LANES), F32)],
        scratch_shapes=[pltpu.VMEM((PROJ_ROWS + CONV_HALO, A_QKV), F32)],
        compiler_params=_params("arbitrary"),
        name="a_in_proj",
    )(x, g.reshape(1, D_MODEL), w_main, w_ba, conv_w)


def _mm(a, b):
    return jnp.dot(a.astype(BF16), b.astype(BF16), preferred_element_type=F32)


def _mm_nt(a, b):
    return lax.dot_general(a.astype(BF16), b.astype(BF16), (((1,), (1,)), ((), ())), preferred_element_type=F32)


def _unit_lower_inverses(mats, xor):
    c = mats[0].shape[0]
    ps = [jnp.where(xor < INV_BASE, -a, 0.0) for a in mats]
    ts = [jnp.where(xor == 0, 1.0, 0.0) + p for p in ps]
    k = 2
    while k < INV_BASE:
        ps = [_mm(p, p) for p in ps]
        ts = [t + _mm(t, p) for t, p in zip(ts, ps)]
        k *= 2
    s = INV_BASE
    while s < c:
        off = (xor >= s) & (xor < 2 * s)
        ats = [_mm(jnp.where(off, a, 0.0), t) for a, t in zip(mats, ts)]
        ts = [t - _mm(t, at) for t, at in zip(ts, ats)]
        s *= 2
    return ts


def _delta_body(main_ref, ba_ref, alog_ref, dtb_ref, onorm_ref, o_ref, state_ref):
    c = CHUNK

    @pl.when(pl.program_id(1) == 0)
    def _():
        state_ref[...] = jnp.zeros(state_ref.shape, F32)

    ba = ba_ref[...]
    beta_all = 1.0 / (1.0 + jnp.exp(-ba))
    z = ba + dtb_ref[...]
    softplus = jnp.maximum(z, 0.0) + jnp.log(1.0 + jnp.exp(-jnp.abs(z)))
    g_all = -jnp.exp(alog_ref[...]) * softplus

    row = lax.broadcasted_iota(jnp.int32, (c, c), 0)
    col = lax.broadcasted_iota(jnp.int32, (c, c), 1)
    lower = row >= col
    strict = row > col
    xor = row ^ col
    tri = jnp.where(lower, 1.0, 0.0)
    gcum_all = jnp.dot(tri, g_all, precision=HIGHEST, preferred_element_type=F32)
    gcum_t = gcum_all.T
    glast_all = gcum_all[c - 1:c, :]
    eg_all = jnp.exp(gcum_all)
    ekd_all = jnp.exp(glast_all - gcum_all)
    elast_all = jnp.exp(glast_all)

    heads = range(A_HEADS)
    qs, ks, vs, betas, e_gs, decays = [], [], [], [], [], []
    for h in heads:
        gl = A_HEADS + h
        qs.append(main_ref[:, h * A_DK:(h + 1) * A_DK])
        ks.append(main_ref[:, A_HEADS * A_DK + h * A_DK:A_HEADS * A_DK + (h + 1) * A_DK])
        vs.append(main_ref[:, 2 * A_HEADS * A_DK + h * A_DV:2 * A_HEADS * A_DK + (h + 1) * A_DV])
        betas.append(beta_all[:, h:h + 1])
        e_gs.append(eg_all[:, gl:gl + 1])
        decays.append(jnp.exp(jnp.where(lower, gcum_all[:, gl:gl + 1] - gcum_t[gl:gl + 1, :], -jnp.inf)))
    k_betas = [k * b for k, b in zip(ks, betas)]
    a_mats = [jnp.where(strict, _mm_nt(kb, k) * d, 0.0) for kb, k, d in zip(k_betas, ks, decays)]
    qks = [_mm_nt(q, k) * d for q, k, d in zip(qs, ks, decays)]
    t_invs = _unit_lower_inverses(a_mats, xor)
    sols = [_mm(t, jnp.concatenate([v * b, kb * e], axis=-1))
            for t, v, b, kb, e in zip(t_invs, vs, betas, k_betas, e_gs)]
    states = [state_ref[h] for h in heads]
    wss = [_mm(jnp.concatenate([sol[:, A_DV:], q * e], axis=0), s)
           for sol, q, e, s in zip(sols, qs, e_gs, states)]
    v_news = [(sol[:, :A_DV] - ws[:c]).astype(BF16) for sol, ws in zip(sols, wss)]
    outs = [ws[c:] + _mm(qk, vn) for ws, qk, vn in zip(wss, qks, v_news)]
    for h in heads:
        gl = A_HEADS + h
        k_dec = ks[h] * ekd_all[:, gl:gl + 1]
        state_ref[h] = states[h] * elast_all[:, gl:gl + 1] + lax.dot_general(
            k_dec.astype(BF16), v_news[h], (((0,), (0,)), ((), ())), preferred_element_type=F32)
    for h in heads:
        gate = main_ref[:, A_QKV + h * A_DV:A_QKV + (h + 1) * A_DV]
        o_ref[:, h * A_DV:(h + 1) * A_DV] = (_rms(outs[h], onorm_ref[...]) * _silu(gate)).astype(BF16)


def _delta_mixer_core(main, ba, a_log, dt_bias, o_norm, batch, seq):
    n_chunks = seq // CHUNK
    lane_pad = lambda p, lo: jnp.pad(p, (lo, LANES - lo - A_HEADS)).reshape(1, LANES)
    rows = lambda b, n: (b * n_chunks + n, 0)
    return pl.pallas_call(
        _delta_body,
        grid=(batch, n_chunks),
        in_specs=[pl.BlockSpec((CHUNK, A_MAIN), rows), pl.BlockSpec((CHUNK, LANES), rows),
                  _const_spec((1, LANES)), _const_spec((1, LANES)), _const_spec((1, A_DV))],
        out_specs=pl.BlockSpec((CHUNK, A_HEADS * A_DV), rows),
        out_shape=jax.ShapeDtypeStruct((batch * seq, A_HEADS * A_DV), BF16),
        scratch_shapes=[pltpu.VMEM((A_HEADS, A_DK, A_DV), F32)],
        compiler_params=_params("parallel", "arbitrary"),
        name="delta_rule",
    )(main, ba, lane_pad(a_log, A_HEADS), lane_pad(dt_bias, A_HEADS), o_norm.reshape(1, A_DV))


def _swa_body(sink_ref, x_ref, g_ref, wq_ref, bq_ref, kvp_ref, kvc_ref, wo_ref, y_ref, q_ref, o_ref):
    first_tile = pl.program_id(1) == 0
    kv_half = B_KV_HEADS * B_DH
    x = x_ref[...]
    h = _rms(x, g_ref[...]).astype(BF16)
    q_ref[...] = ((jnp.dot(h, wq_ref[...], preferred_element_type=F32) + bq_ref[...]) * (B_DH ** -0.5)).astype(BF16)

    qi = lax.broadcasted_iota(jnp.int32, (WINDOW, 2 * WINDOW), 0)
    ki = lax.broadcasted_iota(jnp.int32, (WINDOW, 2 * WINDOW), 1)
    in_win = (ki > qi) & (ki <= qi + WINDOW)
    bias_inner = jnp.where(in_win, 0.0, -jnp.inf)
    bias_first = jnp.where(in_win & (jnp.logical_not(first_tile) | (ki >= WINDOW)), 0.0, -jnp.inf)
    items = [(sub, hq) for sub in range(SWA_ROWS // WINDOW) for hq in range(B_Q_HEADS)]
    keys = [jnp.concatenate([kvp_ref[...], kvc_ref[0:WINDOW, :]], axis=0)]
    keys += [kvc_ref[(sub - 1) * WINDOW:(sub + 1) * WINDOW, :] for sub in range(1, SWA_ROWS // WINDOW)]

    pad = jnp.zeros((2 * WINDOW, B_DH), BF16)
    one = jnp.ones((2 * WINDOW, B_DH), BF16)
    v_ext = [[jnp.concatenate([k[:, kv_half + hk * B_DH:kv_half + (hk + 1) * B_DH], pad, one, pad], axis=1)
              for hk in range(B_KV_HEADS)] for k in keys]

    def scores_of(item):
        sub, hq = item
        hk = hq // B_GROUP
        s = lax.dot_general(q_ref[sub * WINDOW:(sub + 1) * WINDOW, hq * B_DH:(hq + 1) * B_DH],
                            keys[sub][:, hk * B_DH:(hk + 1) * B_DH], (((1,), (1,)), ((), ())),
                            preferred_element_type=F32)
        return s + (bias_first if sub == 0 else bias_inner)

    for g0 in range(0, len(items), SWA_GROUP):
        group = items[g0:g0 + SWA_GROUP]
        scores = [scores_of(it) for it in group]
        sinks = [sink_ref[hq] for _, hq in group]
        maxes = [jnp.maximum(jnp.max(s, axis=-1, keepdims=True), sk) for s, sk in zip(scores, sinks)]
        probs = [jnp.exp(s - m) for s, m in zip(scores, maxes)]
        outs = [jnp.dot(p.astype(BF16), v_ext[sub][hq // B_GROUP], preferred_element_type=F32)
                for (sub, hq), p in zip(group, probs)]
        for (sub, hq), o, sk, m in zip(group, outs, sinks, maxes):
            denom = o[:, 2 * B_DH:3 * B_DH] + jnp.exp(sk - m)
            o_ref[sub * WINDOW:(sub + 1) * WINDOW, hq * B_DH:(hq + 1) * B_DH] = (o[:, :B_DH] / denom).astype(BF16)
    y_ref[...] = x + jnp.dot(o_ref[...], wo_ref[...], preferred_element_type=F32)


def _swa_mixer(x, g, w_q, b_q, sinks, w_out, kv, batch, seq):
    tiles = seq // SWA_ROWS
    per_tile = SWA_ROWS // WINDOW
    dq = B_Q_HEADS * B_DH
    row = pl.BlockSpec((SWA_ROWS, D_MODEL), lambda b, i: (b * tiles + i, 0))
    prev = pl.BlockSpec((WINDOW, kv.shape[1]),
                        lambda b, i: ((b * tiles + i) * per_tile - jnp.minimum(i, 1), 0))
    cur = pl.BlockSpec((SWA_ROWS, kv.shape[1]), lambda b, i: (b * tiles + i, 0))
    return pl.pallas_call(
        _swa_body,
        grid=(batch, tiles),
        in_specs=[pl.BlockSpec(memory_space=pltpu.SMEM), row, _const_spec((1, D_MODEL)),
                  _const_spec((D_MODEL, dq)), _const_spec((1, dq)), prev, cur, _const_spec((dq, D_MODEL))],
        out_specs=row,
        out_shape=jax.ShapeDtypeStruct((batch * seq, D_MODEL), F32),
        scratch_shapes=[pltpu.VMEM((SWA_ROWS, dq), BF16), pltpu.VMEM((SWA_ROWS, dq), BF16)],
        compiler_params=_params("parallel", "arbitrary"),
        name="swa_mixer",
    )(sinks, x, g.reshape(1, D_MODEL), w_q.astype(BF16), b_q.reshape(1, dq), kv, kv, w_out.astype(BF16))


def kernel(x, ffn_norm, ffn_w_gate, ffn_w_up, ffn_w_down, mix_norm, a_w_in, a_conv, a_A_log, a_dt_bias,
           a_o_norm, a_w_out, kv_norm, kv_w, kv_b, b_w_q, b_b_q, b_sinks, b_w_out, final_norm):
    batch, seq, d = x.shape
    x = x.reshape(batch * seq, d)
    ffn_w = lambda i, s: (ffn_norm[i, s], ffn_w_gate[i, s], ffn_w_up[i, s], ffn_w_down[i, s])
    kv = None
    for i in range(DEPTH):
        x = _ffn(x, *ffn_w(i, 0))
        if i < N_A_LAYERS:
            main, ba = _a_in_proj(x, mix_norm[i], a_w_in[i], a_conv[i], seq)
            o = _delta_mixer_core(main, ba, a_A_log[i], a_dt_bias[i], a_o_norm[i], batch, seq)
            if i == N_A_LAYERS - 1:
                x, kv = _ffn(x, *ffn_w(i, 1), pre=(o, a_w_out[i]), post="kv", post_args=(kv_norm, kv_w, kv_b))
            else:
                x = _ffn(x, *ffn_w(i, 1), pre=(o, a_w_out[i]))
        else:
            j = i - N_A_LAYERS
            x = _swa_mixer(x, mix_norm[i], b_w_q[j], b_b_q[j], b_sinks[j], b_w_out[j], kv, batch, seq)
            if i == DEPTH - 1:
                x = _ffn(x, *ffn_w(i, 1), post="final", post_args=(final_norm,))
            else:
                x = _ffn(x, *ffn_w(i, 1))
    return x.reshape(batch, seq, d)
```

```python
import functools

import jax
import jax.numpy as jnp
from jax import lax
from jax.experimental import pallas as pl
from jax.experimental.pallas import tpu as pltpu

F32 = jnp.float32
BF16 = jnp.bfloat16
HIGHEST = lax.Precision.HIGHEST

D_MODEL = 1024
D_FF = 2816
DEPTH = 4
N_A_LAYERS = 2
A_HEADS = 8
A_DK = 128
A_DV = 128
CONV_K = 4
CHUNK = 128
INV_BASE = 8
A_QKV = A_HEADS * (2 * A_DK + A_DV)
A_MAIN = A_QKV + A_HEADS * A_DV
B_Q_HEADS = 16
B_KV_HEADS = 4
B_GROUP = B_Q_HEADS // B_KV_HEADS
B_DH = 64
WINDOW = 128
EPS = 1e-6

LANES = 128
CONV_HALO = 8
VMEM_LIMIT = 56 * 1024 * 1024

FFN_ROWS = 1024
FFN_CHUNK = 256
PROJ_ROWS = 512
SWA_ROWS = 512
SWA_GROUP = 8


def _rms(x, g):
    return x * lax.rsqrt(jnp.mean(x * x, axis=-1, keepdims=True) + EPS) * g


def _silu(x):
    return x / (1.0 + jnp.exp(-x))


def _const_spec(shape):
    return pl.BlockSpec(shape, lambda *_: (0,) * len(shape), pipeline_mode=pl.Buffered(1))


def _stack_spec(shape, which):
    return pl.BlockSpec((None, None) + tuple(shape[2:]), lambda *_: tuple(which) + (0, 0),
                        pipeline_mode=pl.Buffered(1))


def _params(*semantics):
    return pltpu.CompilerParams(dimension_semantics=semantics, vmem_limit_bytes=VMEM_LIMIT)


def _swiglu_residual(x, g_ref, wg_ref, wu_ref, wd_ref):
    h = _rms(x, g_ref[...]).astype(BF16)
    acc = x
    for c0 in range(0, D_FF, FFN_CHUNK):
        cols = slice(c0, c0 + FFN_CHUNK)
        gate = jnp.dot(h, wg_ref[:, cols], preferred_element_type=F32)
        up = jnp.dot(h, wu_ref[:, cols], preferred_element_type=F32)
        act = (_silu(gate) * (0.5 * up)).astype(BF16)
        acc = acc + jnp.dot(act, wd_ref[cols, :], preferred_element_type=F32)
    return acc


def _ffn_body(*refs, pre, post):
    it = iter(refs)
    x_ref = next(it)
    if pre:
        o_ref, wo_ref = next(it), next(it)
    g_ref, wg_ref, wu_ref, wd_ref = next(it), next(it), next(it), next(it)
    if post == "final":
        fg_ref = next(it)
    elif post == "kv":
        kg_ref, kw_ref, kb_ref = next(it), next(it), next(it)
    y_ref = next(it)

    x = x_ref[...]
    if pre:
        x = x + jnp.dot(o_ref[...], wo_ref[...], preferred_element_type=F32)
    y = _swiglu_residual(x, g_ref, wg_ref, wu_ref, wd_ref)
    if post == "final":
        y = _rms(y, fg_ref[...])
    y_ref[...] = y
    if post == "kv":
        kv_ref = next(it)
        hk = _rms(y, kg_ref[...]).astype(BF16)
        kv_ref[...] = (jnp.dot(hk, kw_ref[...], preferred_element_type=F32) + kb_ref[...]).astype(BF16)


def _ffn(x, g, wg, wu, wd, which, pre=None, post=None, post_args=()):
    t = x.shape[0]
    row = pl.BlockSpec((FFN_ROWS, D_MODEL), lambda i: (i, 0))
    vec = lambda v: v.reshape(1, v.shape[-1])
    args, specs = [x], [row]
    if pre is not None:
        o, w_out = pre
        args += [o, w_out.astype(BF16)]
        specs += [pl.BlockSpec((FFN_ROWS, o.shape[1]), lambda i: (i, 0)), _const_spec(w_out.shape)]
    args += [vec(g), wg, wu, wd]
    specs += [_const_spec((1, D_MODEL)), _stack_spec(wg.shape, which), _stack_spec(wu.shape, which),
              _stack_spec(wd.shape, which)]
    out_shape = [jax.ShapeDtypeStruct((t, D_MODEL), F32)]
    out_specs = [row]
    if post == "final":
        args += [vec(post_args[0])]
        specs += [_const_spec((1, D_MODEL))]
    elif post == "kv":
        kg, kw, kb = post_args
        n = kw.shape[1]
        args += [vec(kg), kw.astype(BF16), vec(kb)]
        specs += [_const_spec((1, D_MODEL)), _const_spec((D_MODEL, n)), _const_spec((1, n))]
        out_shape.append(jax.ShapeDtypeStruct((t, n), BF16))
        out_specs.append(pl.BlockSpec((FFN_ROWS, n), lambda i: (i, 0)))
    outs = pl.pallas_call(
        functools.partial(_ffn_body, pre=pre is not None, post=post),
        grid=(t // FFN_ROWS,),
        in_specs=specs,
        out_specs=out_specs,
        out_shape=out_shape,
        compiler_params=_params("parallel"),
        name="ffn" + ("_pre" if pre is not None else "") + ("_" + post if post else ""),
    )(*args)
    return outs[0] if len(outs) == 1 else outs


def _a_in_body(x_ref, g_ref, wm_ref, wb_ref, cw_ref, main_ref, ba_ref, hist_ref, *, tiles_per_seq):
    r = PROJ_ROWS

    @pl.when(pl.program_id(0) % tiles_per_seq == 0)
    def _():
        hist_ref[0:CONV_HALO, :] = jnp.zeros((CONV_HALO, A_QKV), F32)

    h = _rms(x_ref[...], g_ref[...]).astype(BF16)
    ba_ref[...] = jnp.dot(h, wb_ref[...], preferred_element_type=F32)
    main_ref[:, A_QKV:] = jnp.dot(h, wm_ref[:, A_QKV:], preferred_element_type=F32)
    hist_ref[CONV_HALO:CONV_HALO + r, :] = jnp.dot(h, wm_ref[:, :A_QKV], preferred_element_type=F32)

    for blk in range(A_QKV // LANES):
        lanes = slice(blk * LANES, (blk + 1) * LANES)
        conv = cw_ref[CONV_K - 1:CONV_K, lanes] * hist_ref[CONV_HALO:CONV_HALO + r, lanes]
        for j in range(CONV_K - 1):
            lag = CONV_K - 1 - j
            conv = conv + cw_ref[j:j + 1, lanes] * hist_ref[CONV_HALO - lag:CONV_HALO - lag + r, lanes]
        y = _silu(conv)
        if blk < 2 * A_HEADS:
            y = y * lax.rsqrt(jnp.sum(y * y, axis=-1, keepdims=True) + EPS)
        if blk < A_HEADS:
            y = y * (A_DK ** -0.5)
        main_ref[:, lanes] = y
    hist_ref[0:CONV_HALO, :] = hist_ref[r:r + CONV_HALO, :]


def _a_in_proj(x, g, w_in, conv_w, seq):
    t = x.shape[0]
    w_main = w_in[:, :A_MAIN].astype(BF16)
    w_ba = jnp.pad(w_in[:, A_MAIN:], ((0, 0), (0, LANES - 2 * A_HEADS))).astype(BF16)
    return pl.pallas_call(
        functools.partial(_a_in_body, tiles_per_seq=seq // PROJ_ROWS),
        grid=(t // PROJ_ROWS,),
        in_specs=[pl.BlockSpec((PROJ_ROWS, D_MODEL), lambda i: (i, 0)), _const_spec((1, D_MODEL)),
                  _const_spec((D_MODEL, A_MAIN)), _const_spec((D_MODEL, LANES)), _const_spec((CONV_K, A_QKV))],
        out_specs=[pl.BlockSpec((PROJ_ROWS, A_MAIN), lambda i: (i, 0)),
                   pl.BlockSpec((PROJ_ROWS, LANES), lambda i: (i, 0))],
        out_shape=[jax.ShapeDtypeStruct((t, A_MAIN), F32), jax.ShapeDtypeStruct((t, LANES), F32)],
        scratch_shapes=[pltpu.VMEM((PROJ_ROWS + CONV_HALO, A_QKV), F32)],
        compiler_params=_params("arbitrary"),
        name="a_in_proj",
    )(x, g.reshape(1, D_MODEL), w_main, w_ba, conv_w)


def _mm(a, b):
    return jnp.dot(a.astype(BF16), b.astype(BF16), preferred_element_type=F32)


def _mm_nt(a, b):
    return lax.dot_general(a.astype(BF16), b.astype(BF16), (((1,), (1,)), ((), ())), preferred_element_type=F32)


def _unit_lower_inverses(mats, xor):
    c = mats[0].shape[0]
    ps = [jnp.where(xor < INV_BASE, -a, 0.0) for a in mats]
    ts = [jnp.where(xor == 0, 1.0, 0.0) + p for p in ps]
    k = 2
    while k < INV_BASE:
        ps = [_mm(p, p) for p in ps]
        ts = [t + _mm(t, p) for t, p in zip(ts, ps)]
        k *= 2
    s = INV_BASE
    while s < c:
        off = (xor >= s) & (xor < 2 * s)
        ats = [_mm(jnp.where(off, a, 0.0), t) for a, t in zip(mats, ts)]
        ts = [t - _mm(t, at) for t, at in zip(ts, ats)]
        s *= 2
    return ts


def _delta_body(main_ref, ba_ref, alog_ref, dtb_ref, onorm_ref, o_ref, state_ref):
    c = CHUNK

    @pl.when(pl.program_id(1) == 0)
    def _():
        state_ref[...] = jnp.zeros(state_ref.shape, F32)

    ba = ba_ref[...]
    beta_all = 1.0 / (1.0 + jnp.exp(-ba))
    z = ba + dtb_ref[...]
    softplus = jnp.maximum(z, 0.0) + jnp.log(1.0 + jnp.exp(-jnp.abs(z)))
    g_all = -jnp.exp(alog_ref[...]) * softplus

    row = lax.broadcasted_iota(jnp.int32, (c, c), 0)
    col = lax.broadcasted_iota(jnp.int32, (c, c), 1)
    lower = row >= col
    strict = row > col
    xor = row ^ col
    tri = jnp.where(lower, 1.0, 0.0)
    gcum_all = jnp.dot(tri, g_all, precision=HIGHEST, preferred_element_type=F32)
    gcum_t = gcum_all.T
    glast_all = gcum_all[c - 1:c, :]
    eg_all = jnp.exp(gcum_all)
    ekd_all = jnp.exp(glast_all - gcum_all)
    elast_all = jnp.exp(glast_all)

    heads = range(A_HEADS)
    qs, ks, vs, betas, e_gs, decays = [], [], [], [], [], []
    for h in heads:
        gl = A_HEADS + h
        qs.append(main_ref[:, h * A_DK:(h + 1) * A_DK])
        ks.append(main_ref[:, A_HEADS * A_DK + h * A_DK:A_HEADS * A_DK + (h + 1) * A_DK])
        vs.append(main_ref[:, 2 * A_HEADS * A_DK + h * A_DV:2 * A_HEADS * A_DK + (h + 1) * A_DV])
        betas.append(beta_all[:, h:h + 1])
        e_gs.append(eg_all[:, gl:gl + 1])
        decays.append(jnp.exp(jnp.where(lower, gcum_all[:, gl:gl + 1] - gcum_t[gl:gl + 1, :], -jnp.inf)))
    k_betas = [k * b for k, b in zip(ks, betas)]
    a_mats = [jnp.where(strict, _mm_nt(kb, k) * d, 0.0) for kb, k, d in zip(k_betas, ks, decays)]
    qks = [_mm_nt(q, k) * d for q, k, d in zip(qs, ks, decays)]
    t_invs = _unit_lower_inverses(a_mats, xor)
    sols = [_mm(t, jnp.concatenate([v * b, kb * e], axis=-1))
            for t, v, b, kb, e in zip(t_invs, vs, betas, k_betas, e_gs)]
    states = [state_ref[h] for h in heads]
    wss = [_mm(jnp.concatenate([sol[:, A_DV:], q * e], axis=0), s)
           for sol, q, e, s in zip(sols, qs, e_gs, states)]
    v_news = [(sol[:, :A_DV] - ws[:c]).astype(BF16) for sol, ws in zip(sols, wss)]
    outs = [ws[c:] + _mm(qk, vn) for ws, qk, vn in zip(wss, qks, v_news)]
    for h in heads:
        gl = A_HEADS + h
        k_dec = ks[h] * ekd_all[:, gl:gl + 1]
        state_ref[h] = states[h] * elast_all[:, gl:gl + 1] + lax.dot_general(
            k_dec.astype(BF16), v_news[h], (((0,), (0,)), ((), ())), preferred_element_type=F32)
    for h in heads:
        gate = main_ref[:, A_QKV + h * A_DV:A_QKV + (h + 1) * A_DV]
        o_ref[:, h * A_DV:(h + 1) * A_DV] = (_rms(outs[h], onorm_ref[...]) * _silu(gate)).astype(BF16)


def _delta_mixer_core(main, ba, a_log, dt_bias, o_norm, batch, seq):
    n_chunks = seq // CHUNK
    lane_pad = lambda p, lo: jnp.pad(p, (lo, LANES - lo - A_HEADS)).reshape(1, LANES)
    rows = lambda b, n: (b * n_chunks + n, 0)
    return pl.pallas_call(
        _delta_body,
        grid=(batch, n_chunks),
        in_specs=[pl.BlockSpec((CHUNK, A_MAIN), rows), pl.BlockSpec((CHUNK, LANES), rows),
                  _const_spec((1, LANES)), _const_spec((1, LANES)), _const_spec((1, A_DV))],
        out_specs=pl.BlockSpec((CHUNK, A_HEADS * A_DV), rows),
        out_shape=jax.ShapeDtypeStruct((batch * seq, A_HEADS * A_DV), BF16),
        scratch_shapes=[pltpu.VMEM((A_HEADS, A_DK, A_DV), F32)],
        compiler_params=_params("parallel", "arbitrary"),
        name="delta_rule",
    )(main, ba, lane_pad(a_log, A_HEADS), lane_pad(dt_bias, A_HEADS), o_norm.reshape(1, A_DV))


def _swa_body(sink_ref, x_ref, g_ref, wq_ref, bq_ref, kvp_ref, kvc_ref, wo_ref, y_ref, q_ref, o_ref):
    first_tile = pl.program_id(1) == 0
    kv_half = B_KV_HEADS * B_DH
    x = x_ref[...]
    h = _rms(x, g_ref[...]).astype(BF16)
    q_ref[...] = ((jnp.dot(h, wq_ref[...], preferred_element_type=F32) + bq_ref[...]) * (B_DH ** -0.5)).astype(BF16)

    qi = lax.broadcasted_iota(jnp.int32, (WINDOW, 2 * WINDOW), 0)
    ki = lax.broadcasted_iota(jnp.int32, (WINDOW, 2 * WINDOW), 1)
    in_win = (ki > qi) & (ki <= qi + WINDOW)
    bias_inner = jnp.where(in_win, 0.0, -jnp.inf)
    bias_first = jnp.where(in_win & (jnp.logical_not(first_tile) | (ki >= WINDOW)), 0.0, -jnp.inf)
    items = [(sub, hq) for sub in range(SWA_ROWS // WINDOW) for hq in range(B_Q_HEADS)]
    keys = [jnp.concatenate([kvp_ref[...], kvc_ref[0:WINDOW, :]], axis=0)]
    keys += [kvc_ref[(sub - 1) * WINDOW:(sub + 1) * WINDOW, :] for sub in range(1, SWA_ROWS // WINDOW)]

    pad = jnp.zeros((2 * WINDOW, B_DH), BF16)
    one = jnp.ones((2 * WINDOW, B_DH), BF16)
    v_ext = [[jnp.concatenate([k[:, kv_half + hk * B_DH:kv_half + (hk + 1) * B_DH], pad, one, pad], axis=1)
              for hk in range(B_KV_HEADS)] for k in keys]

    def scores_of(item):
        sub, hq = item
        hk = hq // B_GROUP
        s = lax.dot_general(q_ref[sub * WINDOW:(sub + 1) * WINDOW, hq * B_DH:(hq + 1) * B_DH],
                            keys[sub][:, hk * B_DH:(hk + 1) * B_DH], (((1,), (1,)), ((), ())),
                            preferred_element_type=F32)
        return s + (bias_first if sub == 0 else bias_inner)

    for g0 in range(0, len(items), SWA_GROUP):
        group = items[g0:g0 + SWA_GROUP]
        scores = [scores_of(it) for it in group]
        sinks = [sink_ref[hq] for _, hq in group]
        maxes = [jnp.maximum(jnp.max(s, axis=-1, keepdims=True), sk) for s, sk in zip(scores, sinks)]
        probs = [jnp.exp(s - m) for s, m in zip(scores, maxes)]
        outs = [jnp.dot(p.astype(BF16), v_ext[sub][hq // B_GROUP], preferred_element_type=F32)
                for (sub, hq), p in zip(group, probs)]
        for (sub, hq), o, sk, m in zip(group, outs, sinks, maxes):
            denom = o[:, 2 * B_DH:3 * B_DH] + jnp.exp(sk - m)
            o_ref[sub * WINDOW:(sub + 1) * WINDOW, hq * B_DH:(hq + 1) * B_DH] = (o[:, :B_DH] / denom).astype(BF16)
    y_ref[...] = x + jnp.dot(o_ref[...], wo_ref[...], preferred_element_type=F32)


def _swa_mixer(x, g, w_q, b_q, sinks, w_out, kv, batch, seq):
    tiles = seq // SWA_ROWS
    per_tile = SWA_ROWS // WINDOW
    dq = B_Q_HEADS * B_DH
    row = pl.BlockSpec((SWA_ROWS, D_MODEL), lambda b, i: (b * tiles + i, 0))
    prev = pl.BlockSpec((WINDOW, kv.shape[1]),
                        lambda b, i: ((b * tiles + i) * per_tile - jnp.minimum(i, 1), 0))
    cur = pl.BlockSpec((SWA_ROWS, kv.shape[1]), lambda b, i: (b * tiles + i, 0))
    return pl.pallas_call(
        _swa_body,
        grid=(batch, tiles),
        in_specs=[pl.BlockSpec(memory_space=pltpu.SMEM), row, _const_spec((1, D_MODEL)),
                  _const_spec((D_MODEL, dq)), _const_spec((1, dq)), prev, cur, _const_spec((dq, D_MODEL))],
        out_specs=row,
        out_shape=jax.ShapeDtypeStruct((batch * seq, D_MODEL), F32),
        scratch_shapes=[pltpu.VMEM((SWA_ROWS, dq), BF16), pltpu.VMEM((SWA_ROWS, dq), BF16)],
        compiler_params=_params("parallel", "arbitrary"),
        name="swa_mixer",
    )(sinks, x, g.reshape(1, D_MODEL), w_q.astype(BF16), b_q.reshape(1, dq), kv, kv, w_out.astype(BF16))


def kernel(x, ffn_norm, ffn_w_gate, ffn_w_up, ffn_w_down, mix_norm, a_w_in, a_conv, a_A_log, a_dt_bias,
           a_o_norm, a_w_out, kv_norm, kv_w, kv_b, b_w_q, b_b_q, b_sinks, b_w_out, final_norm):
    batch, seq, d = x.shape
    x = x.reshape(batch * seq, d)
    wg, wu, wd = ffn_w_gate.astype(BF16), ffn_w_up.astype(BF16), ffn_w_down.astype(BF16)
    ffn_w = lambda i, s: (ffn_norm[i, s], wg, wu, wd, (i, s))
    kv = None
    for i in range(DEPTH):
        x = _ffn(x, *ffn_w(i, 0))
        if i < N_A_LAYERS:
            main, ba = _a_in_proj(x, mix_norm[i], a_w_in[i], a_conv[i], seq)
            o = _delta_mixer_core(main, ba, a_A_log[i], a_dt_bias[i], a_o_norm[i], batch, seq)
            if i == N_A_LAYERS - 1:
                x, kv = _ffn(x, *ffn_w(i, 1), pre=(o, a_w_out[i]), post="kv", post_args=(kv_norm, kv_w, kv_b))
            else:
                x = _ffn(x, *ffn_w(i, 1), pre=(o, a_w_out[i]))
        else:
            j = i - N_A_LAYERS
            x = _swa_mixer(x, mix_norm[i], b_w_q[j], b_b_q[j], b_sinks[j], b_w_out[j], kv, batch, seq)
            if i == DEPTH - 1:
                x = _ffn(x, *ffn_w(i, 1), post="final", post_args=(final_norm,))
            else:
                x = _ffn(x, *ffn_w(i, 1))
    return x.reshape(batch, seq, d)
```

```python
import functools

import jax
import jax.numpy as jnp
from jax import lax
from jax.experimental import pallas as pl
from jax.experimental.pallas import tpu as pltpu

F32 = jnp.float32
BF16 = jnp.bfloat16
HIGHEST = lax.Precision.HIGHEST

D_MODEL = 1024
D_FF = 2816
DEPTH = 4
N_A_LAYERS = 2
A_HEADS = 8
A_DK = 128
A_DV = 128
CONV_K = 4
CHUNK = 128
DELTA_SEQS = 2
INV_BASE = 8
A_QKV = A_HEADS * (2 * A_DK + A_DV)
A_MAIN = A_QKV + A_HEADS * A_DV
B_Q_HEADS = 16
B_KV_HEADS = 4
B_GROUP = B_Q_HEADS // B_KV_HEADS
B_DH = 64
WINDOW = 128
EPS = 1e-6

LANES = 128
CONV_HALO = 8
VMEM_LIMIT = 56 * 1024 * 1024

FFN_ROWS = 1024
FFN_CHUNK = 256
PROJ_ROWS = 512
SWA_ROWS = 512
SWA_GROUP = 8


def _rms(x, g):
    return x * lax.rsqrt(jnp.mean(x * x, axis=-1, keepdims=True) + EPS) * g


def _silu(x):
    return x / (1.0 + jnp.exp(-x))


def _const_spec(shape):
    return pl.BlockSpec(shape, lambda *_: (0,) * len(shape), pipeline_mode=pl.Buffered(1))


def _stack_spec(shape, which):
    return pl.BlockSpec((None, None) + tuple(shape[2:]), lambda *_: tuple(which) + (0, 0),
                        pipeline_mode=pl.Buffered(1))


def _params(*semantics):
    return pltpu.CompilerParams(dimension_semantics=semantics, vmem_limit_bytes=VMEM_LIMIT)


def _swiglu_residual(x, g_ref, wg_ref, wu_ref, wd_ref):
    h = _rms(x, g_ref[...]).astype(BF16)
    acc = x
    for c0 in range(0, D_FF, FFN_CHUNK):
        cols = slice(c0, c0 + FFN_CHUNK)
        gate = jnp.dot(h, wg_ref[:, cols], preferred_element_type=F32)
        up = jnp.dot(h, wu_ref[:, cols], preferred_element_type=F32)
        act = (_silu(gate) * (0.5 * up)).astype(BF16)
        acc = acc + jnp.dot(act, wd_ref[cols, :], preferred_element_type=F32)
    return acc


def _ffn_body(*refs, pre, post):
    it = iter(refs)
    x_ref = next(it)
    if pre:
        o_ref, wo_ref = next(it), next(it)
    g_ref, wg_ref, wu_ref, wd_ref = next(it), next(it), next(it), next(it)
    if post == "final":
        fg_ref = next(it)
    elif post == "kv":
        kg_ref, kw_ref, kb_ref = next(it), next(it), next(it)
    y_ref = next(it)

    x = x_ref[...]
    if pre:
        x = x + jnp.dot(o_ref[...], wo_ref[...], preferred_element_type=F32)
    y = _swiglu_residual(x, g_ref, wg_ref, wu_ref, wd_ref)
    if post == "final":
        y = _rms(y, fg_ref[...])
    y_ref[...] = y
    if post == "kv":
        kv_ref = next(it)
        hk = _rms(y, kg_ref[...]).astype(BF16)
        kv_ref[...] = (jnp.dot(hk, kw_ref[...], preferred_element_type=F32) + kb_ref[...]).astype(BF16)


def _ffn(x, g, wg, wu, wd, which, pre=None, post=None, post_args=()):
    t = x.shape[0]
    row = pl.BlockSpec((FFN_ROWS, D_MODEL), lambda i: (i, 0))
    vec = lambda v: v.reshape(1, v.shape[-1])
    args, specs = [x], [row]
    if pre is not None:
        o, w_out = pre
        args += [o, w_out.astype(BF16)]
        specs += [pl.BlockSpec((FFN_ROWS, o.shape[1]), lambda i: (i, 0)), _const_spec(w_out.shape)]
    args += [vec(g), wg, wu, wd]
    specs += [_const_spec((1, D_MODEL)), _stack_spec(wg.shape, which), _stack_spec(wu.shape, which),
              _stack_spec(wd.shape, which)]
    out_shape = [jax.ShapeDtypeStruct((t, D_MODEL), F32)]
    out_specs = [row]
    if post == "final":
        args += [vec(post_args[0])]
        specs += [_const_spec((1, D_MODEL))]
    elif post == "kv":
        kg, kw, kb = post_args
        n = kw.shape[1]
        args += [vec(kg), kw.astype(BF16), vec(kb)]
        specs += [_const_spec((1, D_MODEL)), _const_spec((D_MODEL, n)), _const_spec((1, n))]
        out_shape.append(jax.ShapeDtypeStruct((t, n), BF16))
        out_specs.append(pl.BlockSpec((FFN_ROWS, n), lambda i: (i, 0)))
    outs = pl.pallas_call(
        functools.partial(_ffn_body, pre=pre is not None, post=post),
        grid=(t // FFN_ROWS,),
        in_specs=specs,
        out_specs=out_specs,
        out_shape=out_shape,
        compiler_params=_params("parallel"),
        name="ffn" + ("_pre" if pre is not None else "") + ("_" + post if post else ""),
    )(*args)
    return outs[0] if len(outs) == 1 else outs


def _a_in_body(x_ref, g_ref, wm_ref, wb_ref, cw_ref, main_ref, ba_ref, hist_ref, *, tiles_per_seq):
    r = PROJ_ROWS

    @pl.when(pl.program_id(0) % tiles_per_seq == 0)
    def _():
        hist_ref[0:CONV_HALO, :] = jnp.zeros((CONV_HALO, A_QKV), F32)

    h = _rms(x_ref[...], g_ref[...]).astype(BF16)
    ba_ref[...] = jnp.dot(h, wb_ref[...], preferred_element_type=F32)
    main_ref[:, A_QKV:] = jnp.dot(h, wm_ref[:, A_QKV:], preferred_element_type=F32)
    hist_ref[CONV_HALO:CONV_HALO + r, :] = jnp.dot(h, wm_ref[:, :A_QKV], preferred_element_type=F32)

    for blk in range(A_QKV // LANES):
        lanes = slice(blk * LANES, (blk + 1) * LANES)
        conv = cw_ref[CONV_K - 1:CONV_K, lanes] * hist_ref[CONV_HALO:CONV_HALO + r, lanes]
        for j in range(CONV_K - 1):
            lag = CONV_K - 1 - j
            conv = conv + cw_ref[j:j + 1, lanes] * hist_ref[CONV_HALO - lag:CONV_HALO - lag + r, lanes]
        y = _silu(conv)
        if blk < 2 * A_HEADS:
            y = y * lax.rsqrt(jnp.sum(y * y, axis=-1, keepdims=True) + EPS)
        if blk < A_HEADS:
            y = y * (A_DK ** -0.5)
        main_ref[:, lanes] = y
    hist_ref[0:CONV_HALO, :] = hist_ref[r:r + CONV_HALO, :]


def _a_in_proj(x, g, w_in, conv_w, seq):
    t = x.shape[0]
    w_main = w_in[:, :A_MAIN].astype(BF16)
    w_ba = jnp.pad(w_in[:, A_MAIN:], ((0, 0), (0, LANES - 2 * A_HEADS))).astype(BF16)
    return pl.pallas_call(
        functools.partial(_a_in_body, tiles_per_seq=seq // PROJ_ROWS),
        grid=(t // PROJ_ROWS,),
        in_specs=[pl.BlockSpec((PROJ_ROWS, D_MODEL), lambda i: (i, 0)), _const_spec((1, D_MODEL)),
                  _const_spec((D_MODEL, A_MAIN)), _const_spec((D_MODEL, LANES)), _const_spec((CONV_K, A_QKV))],
        out_specs=[pl.BlockSpec((PROJ_ROWS, A_MAIN), lambda i: (i, 0)),
                   pl.BlockSpec((PROJ_ROWS, LANES), lambda i: (i, 0))],
        out_shape=[jax.ShapeDtypeStruct((t, A_MAIN), F32), jax.ShapeDtypeStruct((t, LANES), F32)],
        scratch_shapes=[pltpu.VMEM((PROJ_ROWS + CONV_HALO, A_QKV), F32)],
        compiler_params=_params("arbitrary"),
        name="a_in_proj",
    )(x, g.reshape(1, D_MODEL), w_main, w_ba, conv_w)


def _mm(a, b):
    return jnp.dot(a.astype(BF16), b.astype(BF16), preferred_element_type=F32)


def _mm_nt(a, b):
    return lax.dot_general(a.astype(BF16), b.astype(BF16), (((1,), (1,)), ((), ())), preferred_element_type=F32)


def _unit_lower_inverses(mats, xor):
    c = mats[0].shape[0]
    ps = [jnp.where(xor < INV_BASE, -a, 0.0) for a in mats]
    ts = [jnp.where(xor == 0, 1.0, 0.0) + p for p in ps]
    k = 2
    while k < INV_BASE:
        ps = [_mm(p, p) for p in ps]
        ts = [t + _mm(t, p) for t, p in zip(ts, ps)]
        k *= 2
    s = INV_BASE
    while s < c:
        off = (xor >= s) & (xor < 2 * s)
        ats = [_mm(jnp.where(off, a, 0.0), t) for a, t in zip(mats, ts)]
        ts = [t - _mm(t, at) for t, at in zip(ts, ats)]
        s *= 2
    return ts


def _delta_body(main_ref, ba_ref, alog_ref, dtb_ref, onorm_ref, o_ref, state_ref):
    c = CHUNK

    @pl.when(pl.program_id(1) == 0)
    def _():
        state_ref[...] = jnp.zeros(state_ref.shape, F32)

    row = lax.broadcasted_iota(jnp.int32, (c, c), 0)
    col = lax.broadcasted_iota(jnp.int32, (c, c), 1)
    lower = row >= col
    strict = row > col
    xor = row ^ col
    tri = jnp.where(lower, 1.0, 0.0)

    pairs = [(s, h) for s in range(DELTA_SEQS) for h in range(A_HEADS)]
    gates = []
    for s in range(DELTA_SEQS):
        ba = ba_ref[s]
        beta_all = 1.0 / (1.0 + jnp.exp(-ba))
        z = ba + dtb_ref[...]
        softplus = jnp.maximum(z, 0.0) + jnp.log(1.0 + jnp.exp(-jnp.abs(z)))
        g_all = -jnp.exp(alog_ref[...]) * softplus
        gcum_all = jnp.dot(tri, g_all, precision=HIGHEST, preferred_element_type=F32)
        glast_all = gcum_all[c - 1:c, :]
        gates.append(dict(beta=beta_all, gcum=gcum_all, gcum_t=gcum_all.T, eg=jnp.exp(gcum_all),
                          ekd=jnp.exp(glast_all - gcum_all), elast=jnp.exp(glast_all)))

    qs, ks, vs, betas, e_gs, decays = [], [], [], [], [], []
    for s, h in pairs:
        gl = A_HEADS + h
        gt = gates[s]
        qs.append(main_ref[s, :, h * A_DK:(h + 1) * A_DK])
        ks.append(main_ref[s, :, A_HEADS * A_DK + h * A_DK:A_HEADS * A_DK + (h + 1) * A_DK])
        vs.append(main_ref[s, :, 2 * A_HEADS * A_DK + h * A_DV:2 * A_HEADS * A_DK + (h + 1) * A_DV])
        betas.append(gt["beta"][:, h:h + 1])
        e_gs.append(gt["eg"][:, gl:gl + 1])
        decays.append(jnp.exp(jnp.where(lower, gt["gcum"][:, gl:gl + 1] - gt["gcum_t"][gl:gl + 1, :], -jnp.inf)))
    k_betas = [k * b for k, b in zip(ks, betas)]
    a_mats = [jnp.where(strict, _mm_nt(kb, k) * d, 0.0) for kb, k, d in zip(k_betas, ks, decays)]
    qks = [_mm_nt(q, k) * d for q, k, d in zip(qs, ks, decays)]
    t_invs = _unit_lower_inverses(a_mats, xor)
    sols = [_mm(t, jnp.concatenate([v * b, kb * e], axis=-1))
            for t, v, b, kb, e in zip(t_invs, vs, betas, k_betas, e_gs)]
    states = [state_ref[s, h] for s, h in pairs]
    wss = [_mm(jnp.concatenate([sol[:, A_DV:], q * e], axis=0), st)
           for sol, q, e, st in zip(sols, qs, e_gs, states)]
    v_news = [(sol[:, :A_DV] - ws[:c]).astype(BF16) for sol, ws in zip(sols, wss)]
    outs = [ws[c:] + _mm(qk, vn) for ws, qk, vn in zip(wss, qks, v_news)]
    for i, (s, h) in enumerate(pairs):
        gl = A_HEADS + h
        k_dec = ks[i] * gates[s]["ekd"][:, gl:gl + 1]
        state_ref[s, h] = states[i] * gates[s]["elast"][:, gl:gl + 1] + lax.dot_general(
            k_dec.astype(BF16), v_news[i], (((0,), (0,)), ((), ())), preferred_element_type=F32)
    for i, (s, h) in enumerate(pairs):
        gate = main_ref[s, :, A_QKV + h * A_DV:A_QKV + (h + 1) * A_DV]
        o_ref[s, :, h * A_DV:(h + 1) * A_DV] = (_rms(outs[i], onorm_ref[...]) * _silu(gate)).astype(BF16)


def _delta_mixer_core(main, ba, a_log, dt_bias, o_norm, batch, seq):
    n_chunks = seq // CHUNK
    lane_pad = lambda p, lo: jnp.pad(p, (lo, LANES - lo - A_HEADS)).reshape(1, LANES)
    rows = lambda b, n: (b, n, 0)
    per_seq = lambda a: a.reshape(batch, seq, a.shape[-1])
    out = pl.pallas_call(
        _delta_body,
        grid=(batch // DELTA_SEQS, n_chunks),
        in_specs=[pl.BlockSpec((DELTA_SEQS, CHUNK, A_MAIN), rows), pl.BlockSpec((DELTA_SEQS, CHUNK, LANES), rows),
                  _const_spec((1, LANES)), _const_spec((1, LANES)), _const_spec((1, A_DV))],
        out_specs=pl.BlockSpec((DELTA_SEQS, CHUNK, A_HEADS * A_DV), rows),
        out_shape=jax.ShapeDtypeStruct((batch, seq, A_HEADS * A_DV), BF16),
        scratch_shapes=[pltpu.VMEM((DELTA_SEQS, A_HEADS, A_DK, A_DV), F32)],
        compiler_params=_params("parallel", "arbitrary"),
        name="delta_rule",
    )(per_seq(main), per_seq(ba), lane_pad(a_log, A_HEADS), lane_pad(dt_bias, A_HEADS), o_norm.reshape(1, A_DV))
    return out.reshape(batch * seq, A_HEADS * A_DV)


def _swa_body(sink_ref, x_ref, g_ref, wq_ref, bq_ref, kvp_ref, kvc_ref, wo_ref, y_ref, q_ref, o_ref):
    first_tile = pl.program_id(1) == 0
    kv_half = B_KV_HEADS * B_DH
    x = x_ref[...]
    h = _rms(x, g_ref[...]).astype(BF16)
    q_ref[...] = ((jnp.dot(h, wq_ref[...], preferred_element_type=F32) + bq_ref[...]) * (B_DH ** -0.5)).astype(BF16)

    qi = lax.broadcasted_iota(jnp.int32, (WINDOW, 2 * WINDOW), 0)
    ki = lax.broadcasted_iota(jnp.int32, (WINDOW, 2 * WINDOW), 1)
    in_win = (ki > qi) & (ki <= qi + WINDOW)
    bias_inner = jnp.where(in_win, 0.0, -jnp.inf)
    bias_first = jnp.where(in_win & (jnp.logical_not(first_tile) | (ki >= WINDOW)), 0.0, -jnp.inf)
    items = [(sub, hq) for sub in range(SWA_ROWS // WINDOW) for hq in range(B_Q_HEADS)]
    keys = [jnp.concatenate([kvp_ref[...], kvc_ref[0:WINDOW, :]], axis=0)]
    keys += [kvc_ref[(sub - 1) * WINDOW:(sub + 1) * WINDOW, :] for sub in range(1, SWA_ROWS // WINDOW)]

    pad = jnp.zeros((2 * WINDOW, B_DH), BF16)
    one = jnp.ones((2 * WINDOW, B_DH), BF16)
    v_ext = [[jnp.concatenate([k[:, kv_half + hk * B_DH:kv_half + (hk + 1) * B_DH], pad, one, pad], axis=1)
              for hk in range(B_KV_HEADS)] for k in keys]

    def scores_of(item):
        sub, hq = item
        hk = hq // B_GROUP
        s = lax.dot_general(q_ref[sub * WINDOW:(sub + 1) * WINDOW, hq * B_DH:(hq + 1) * B_DH],
                            keys[sub][:, hk * B_DH:(hk + 1) * B_DH], (((1,), (1,)), ((), ())),
                            preferred_element_type=F32)
        return s + (bias_first if sub == 0 else bias_inner)

    for g0 in range(0, len(items), SWA_GROUP):
        group = items[g0:g0 + SWA_GROUP]
        scores = [scores_of(it) for it in group]
        sinks = [sink_ref[hq] for _, hq in group]
        maxes = [jnp.maximum(jnp.max(s, axis=-1, keepdims=True), sk) for s, sk in zip(scores, sinks)]
        probs = [jnp.exp(s - m) for s, m in zip(scores, maxes)]
        outs = [jnp.dot(p.astype(BF16), v_ext[sub][hq // B_GROUP], preferred_element_type=F32)
                for (sub, hq), p in zip(group, probs)]
        for (sub, hq), o, sk, m in zip(group, outs, sinks, maxes):
            denom = o[:, 2 * B_DH:3 * B_DH] + jnp.exp(sk - m)
            o_ref[sub * WINDOW:(sub + 1) * WINDOW, hq * B_DH:(hq + 1) * B_DH] = (o[:, :B_DH] / denom).astype(BF16)
    y_ref[...] = x + jnp.dot(o_ref[...], wo_ref[...], preferred_element_type=F32)


def _swa_mixer(x, g, w_q, b_q, sinks, w_out, kv, batch, seq):
    tiles = seq // SWA_ROWS
    per_tile = SWA_ROWS // WINDOW
    dq = B_Q_HEADS * B_DH
    row = pl.BlockSpec((SWA_ROWS, D_MODEL), lambda b, i: (b * tiles + i, 0))
    prev = pl.BlockSpec((WINDOW, kv.shape[1]),
                        lambda b, i: ((b * tiles + i) * per_tile - jnp.minimum(i, 1), 0))
    cur = pl.BlockSpec((SWA_ROWS, kv.shape[1]), lambda b, i: (b * tiles + i, 0))
    return pl.pallas_call(
        _swa_body,
        grid=(batch, tiles),
        in_specs=[pl.BlockSpec(memory_space=pltpu.SMEM), row, _const_spec((1, D_MODEL)),
                  _const_spec((D_MODEL, dq)), _const_spec((1, dq)), prev, cur, _const_spec((dq, D_MODEL))],
        out_specs=row,
        out_shape=jax.ShapeDtypeStruct((batch * seq, D_MODEL), F32),
        scratch_shapes=[pltpu.VMEM((SWA_ROWS, dq), BF16), pltpu.VMEM((SWA_ROWS, dq), BF16)],
        compiler_params=_params("parallel", "arbitrary"),
        name="swa_mixer",
    )(sinks, x, g.reshape(1, D_MODEL), w_q.astype(BF16), b_q.reshape(1, dq), kv, kv, w_out.astype(BF16))


def kernel(x, ffn_norm, ffn_w_gate, ffn_w_up, ffn_w_down, mix_norm, a_w_in, a_conv, a_A_log, a_dt_bias,
           a_o_norm, a_w_out, kv_norm, kv_w, kv_b, b_w_q, b_b_q, b_sinks, b_w_out, final_norm):
    batch, seq, d = x.shape
    x = x.reshape(batch * seq, d)
    wg, wu, wd = ffn_w_gate.astype(BF16), ffn_w_up.astype(BF16), ffn_w_down.astype(BF16)
    ffn_w = lambda i, s: (ffn_norm[i, s], wg, wu, wd, (i, s))
    kv = None
    for i in range(DEPTH):
        x = _ffn(x, *ffn_w(i, 0))
        if i < N_A_LAYERS:
            main, ba = _a_in_proj(x, mix_norm[i], a_w_in[i], a_conv[i], seq)
            o = _delta_mixer_core(main, ba, a_A_log[i], a_dt_bias[i], a_o_norm[i], batch, seq)
            if i == N_A_LAYERS - 1:
                x, kv = _ffn(x, *ffn_w(i, 1), pre=(o, a_w_out[i]), post="kv", post_args=(kv_norm, kv_w, kv_b))
            else:
                x = _ffn(x, *ffn_w(i, 1), pre=(o, a_w_out[i]))
        else:
            j = i - N_A_LAYERS
            x = _swa_mixer(x, mix_norm[i], b_w_q[j], b_b_q[j], b_sinks[j], b_w_out[j], kv, batch, seq)
            if i == DEPTH - 1:
                x = _ffn(x, *ffn_w(i, 1), post="final", post_args=(final_norm,))
            else:
                x = _ffn(x, *ffn_w(i, 1))
    return x.reshape(batch, seq, d)
```

```python
import functools

import jax
import jax.numpy as jnp
from jax import lax
from jax.experimental import pallas as pl
from jax.experimental.pallas import tpu as pltpu

F32 = jnp.float32
BF16 = jnp.bfloat16
HIGHEST = lax.Precision.HIGHEST

D_MODEL = 1024
D_FF = 2816
DEPTH = 4
N_A_LAYERS = 2
A_HEADS = 8
A_DK = 128
A_DV = 128
CONV_K = 4
CHUNK = 128
DELTA_SEQS = 4
INV_BASE = 8
A_QKV = A_HEADS * (2 * A_DK + A_DV)
A_MAIN = A_QKV + A_HEADS * A_DV
B_Q_HEADS = 16
B_KV_HEADS = 4
B_GROUP = B_Q_HEADS // B_KV_HEADS
B_DH = 64
WINDOW = 128
EPS = 1e-6

LANES = 128
CONV_HALO = 8
VMEM_LIMIT = 56 * 1024 * 1024

FFN_ROWS = 1024
FFN_CHUNK = 256
PROJ_ROWS = 512
SWA_ROWS = 512
SWA_GROUP = 8


def _rms(x, g):
    return x * lax.rsqrt(jnp.mean(x * x, axis=-1, keepdims=True) + EPS) * g


def _silu(x):
    return x / (1.0 + jnp.exp(-x))


def _const_spec(shape):
    return pl.BlockSpec(shape, lambda *_: (0,) * len(shape), pipeline_mode=pl.Buffered(1))


def _stack_spec(shape, which):
    return pl.BlockSpec((None, None) + tuple(shape[2:]), lambda *_: tuple(which) + (0, 0),
                        pipeline_mode=pl.Buffered(1))


def _params(*semantics):
    return pltpu.CompilerParams(dimension_semantics=semantics, vmem_limit_bytes=VMEM_LIMIT)


def _swiglu_residual(x, g_ref, wg_ref, wu_ref, wd_ref):
    h = _rms(x, g_ref[...]).astype(BF16)
    acc = x
    for c0 in range(0, D_FF, FFN_CHUNK):
        cols = slice(c0, c0 + FFN_CHUNK)
        gate = jnp.dot(h, wg_ref[:, cols], preferred_element_type=F32)
        up = jnp.dot(h, wu_ref[:, cols], preferred_element_type=F32)
        act = (_silu(gate) * (0.5 * up)).astype(BF16)
        acc = acc + jnp.dot(act, wd_ref[cols, :], preferred_element_type=F32)
    return acc


def _ffn_body(*refs, pre, post):
    it = iter(refs)
    x_ref = next(it)
    if pre:
        o_ref, wo_ref = next(it), next(it)
    g_ref, wg_ref, wu_ref, wd_ref = next(it), next(it), next(it), next(it)
    if post == "final":
        fg_ref = next(it)
    elif post == "kv":
        kg_ref, kw_ref, kb_ref = next(it), next(it), next(it)
    y_ref = next(it)

    x = x_ref[...]
    if pre:
        x = x + jnp.dot(o_ref[...], wo_ref[...], preferred_element_type=F32)
    y = _swiglu_residual(x, g_ref, wg_ref, wu_ref, wd_ref)
    if post == "final":
        y = _rms(y, fg_ref[...])
    y_ref[...] = y
    if post == "kv":
        kv_ref = next(it)
        hk = _rms(y, kg_ref[...]).astype(BF16)
        kv_ref[...] = (jnp.dot(hk, kw_ref[...], preferred_element_type=F32) + kb_ref[...]).astype(BF16)


def _ffn(x, g, wg, wu, wd, which, pre=None, post=None, post_args=()):
    t = x.shape[0]
    row = pl.BlockSpec((FFN_ROWS, D_MODEL), lambda i: (i, 0))
    vec = lambda v: v.reshape(1, v.shape[-1])
    args, specs = [x], [row]
    if pre is not None:
        o, w_out = pre
        args += [o, w_out.astype(BF16)]
        specs += [pl.BlockSpec((FFN_ROWS, o.shape[1]), lambda i: (i, 0)), _const_spec(w_out.shape)]
    args += [vec(g), wg, wu, wd]
    specs += [_const_spec((1, D_MODEL)), _stack_spec(wg.shape, which), _stack_spec(wu.shape, which),
              _stack_spec(wd.shape, which)]
    out_shape = [jax.ShapeDtypeStruct((t, D_MODEL), F32)]
    out_specs = [row]
    if post == "final":
        args += [vec(post_args[0])]
        specs += [_const_spec((1, D_MODEL))]
    elif post == "kv":
        kg, kw, kb = post_args
        n = kw.shape[1]
        args += [vec(kg), kw.astype(BF16), vec(kb)]
        specs += [_const_spec((1, D_MODEL)), _const_spec((D_MODEL, n)), _const_spec((1, n))]
        out_shape.append(jax.ShapeDtypeStruct((t, n), BF16))
        out_specs.append(pl.BlockSpec((FFN_ROWS, n), lambda i: (i, 0)))
    outs = pl.pallas_call(
        functools.partial(_ffn_body, pre=pre is not None, post=post),
        grid=(t // FFN_ROWS,),
        in_specs=specs,
        out_specs=out_specs,
        out_shape=out_shape,
        compiler_params=_params("parallel"),
        name="ffn" + ("_pre" if pre is not None else "") + ("_" + post if post else ""),
    )(*args)
    return outs[0] if len(outs) == 1 else outs


def _a_in_body(x_ref, g_ref, wm_ref, wb_ref, cw_ref, main_ref, ba_ref, hist_ref, *, tiles_per_seq):
    r = PROJ_ROWS

    @pl.when(pl.program_id(0) % tiles_per_seq == 0)
    def _():
        hist_ref[0:CONV_HALO, :] = jnp.zeros((CONV_HALO, A_QKV), F32)

    h = _rms(x_ref[...], g_ref[...]).astype(BF16)
    ba_ref[...] = jnp.dot(h, wb_ref[...], preferred_element_type=F32)
    main_ref[:, A_QKV:] = jnp.dot(h, wm_ref[:, A_QKV:], preferred_element_type=F32)
    hist_ref[CONV_HALO:CONV_HALO + r, :] = jnp.dot(h, wm_ref[:, :A_QKV], preferred_element_type=F32)

    for blk in range(A_QKV // LANES):
        lanes = slice(blk * LANES, (blk + 1) * LANES)
        conv = cw_ref[CONV_K - 1:CONV_K, lanes] * hist_ref[CONV_HALO:CONV_HALO + r, lanes]
        for j in range(CONV_K - 1):
            lag = CONV_K - 1 - j
            conv = conv + cw_ref[j:j + 1, lanes] * hist_ref[CONV_HALO - lag:CONV_HALO - lag + r, lanes]
        y = _silu(conv)
        if blk < 2 * A_HEADS:
            y = y * lax.rsqrt(jnp.sum(y * y, axis=-1, keepdims=True) + EPS)
        if blk < A_HEADS:
            y = y * (A_DK ** -0.5)
        main_ref[:, lanes] = y
    hist_ref[0:CONV_HALO, :] = hist_ref[r:r + CONV_HALO, :]


def _a_in_proj(x, g, w_in, conv_w, seq):
    t = x.shape[0]
    w_main = w_in[:, :A_MAIN].astype(BF16)
    w_ba = jnp.pad(w_in[:, A_MAIN:], ((0, 0), (0, LANES - 2 * A_HEADS))).astype(BF16)
    return pl.pallas_call(
        functools.partial(_a_in_body, tiles_per_seq=seq // PROJ_ROWS),
        grid=(t // PROJ_ROWS,),
        in_specs=[pl.BlockSpec((PROJ_ROWS, D_MODEL), lambda i: (i, 0)), _const_spec((1, D_MODEL)),
                  _const_spec((D_MODEL, A_MAIN)), _const_spec((D_MODEL, LANES)), _const_spec((CONV_K, A_QKV))],
        out_specs=[pl.BlockSpec((PROJ_ROWS, A_MAIN), lambda i: (i, 0)),
                   pl.BlockSpec((PROJ_ROWS, LANES), lambda i: (i, 0))],
        out_shape=[jax.ShapeDtypeStruct((t, A_MAIN), F32), jax.ShapeDtypeStruct((t, LANES), F32)],
        scratch_shapes=[pltpu.VMEM((PROJ_ROWS + CONV_HALO, A_QKV), F32)],
        compiler_params=_params("arbitrary"),
        name="a_in_proj",
    )(x, g.reshape(1, D_MODEL), w_main, w_ba, conv_w)


def _mm(a, b):
    return jnp.dot(a.astype(BF16), b.astype(BF16), preferred_element_type=F32)


def _mm_nt(a, b):
    return lax.dot_general(a.astype(BF16), b.astype(BF16), (((1,), (1,)), ((), ())), preferred_element_type=F32)


def _unit_lower_inverses(mats, xor):
    c = mats[0].shape[0]
    ps = [jnp.where(xor < INV_BASE, -a, 0.0) for a in mats]
    ts = [jnp.where(xor == 0, 1.0, 0.0) + p for p in ps]
    k = 2
    while k < INV_BASE:
        ps = [_mm(p, p) for p in ps]
        ts = [t + _mm(t, p) for t, p in zip(ts, ps)]
        k *= 2
    s = INV_BASE
    while s < c:
        off = (xor >= s) & (xor < 2 * s)
        ats = [_mm(jnp.where(off, a, 0.0), t) for a, t in zip(mats, ts)]
        ts = [t - _mm(t, at) for t, at in zip(ts, ats)]
        s *= 2
    return ts


def _delta_body(main_ref, ba_ref, alog_ref, dtb_ref, onorm_ref, o_ref, state_ref):
    c = CHUNK

    @pl.when(pl.program_id(1) == 0)
    def _():
        state_ref[...] = jnp.zeros(state_ref.shape, F32)

    row = lax.broadcasted_iota(jnp.int32, (c, c), 0)
    col = lax.broadcasted_iota(jnp.int32, (c, c), 1)
    lower = row >= col
    strict = row > col
    xor = row ^ col
    tri = jnp.where(lower, 1.0, 0.0)

    pairs = [(s, h) for s in range(DELTA_SEQS) for h in range(A_HEADS)]
    gates = []
    for s in range(DELTA_SEQS):
        ba = ba_ref[s]
        beta_all = 1.0 / (1.0 + jnp.exp(-ba))
        z = ba + dtb_ref[...]
        softplus = jnp.maximum(z, 0.0) + jnp.log(1.0 + jnp.exp(-jnp.abs(z)))
        g_all = -jnp.exp(alog_ref[...]) * softplus
        gcum_all = jnp.dot(tri, g_all, precision=HIGHEST, preferred_element_type=F32)
        glast_all = gcum_all[c - 1:c, :]
        gates.append(dict(beta=beta_all, gcum=gcum_all, gcum_t=gcum_all.T, eg=jnp.exp(gcum_all),
                          ekd=jnp.exp(glast_all - gcum_all), elast=jnp.exp(glast_all)))

    qs, ks, vs, betas, e_gs, decays = [], [], [], [], [], []
    for s, h in pairs:
        gl = A_HEADS + h
        gt = gates[s]
        qs.append(main_ref[s, :, h * A_DK:(h + 1) * A_DK])
        ks.append(main_ref[s, :, A_HEADS * A_DK + h * A_DK:A_HEADS * A_DK + (h + 1) * A_DK])
        vs.append(main_ref[s, :, 2 * A_HEADS * A_DK + h * A_DV:2 * A_HEADS * A_DK + (h + 1) * A_DV])
        betas.append(gt["beta"][:, h:h + 1])
        e_gs.append(gt["eg"][:, gl:gl + 1])
        decays.append(jnp.exp(jnp.where(lower, gt["gcum"][:, gl:gl + 1] - gt["gcum_t"][gl:gl + 1, :], -jnp.inf)))
    k_betas = [k * b for k, b in zip(ks, betas)]
    kq = [_mm_nt(jnp.concatenate([kb, q], axis=0), k) for kb, q, k in zip(k_betas, qs, ks)]
    a_mats = [jnp.where(strict, x[:c] * d, 0.0) for x, d in zip(kq, decays)]
    qks = [x[c:] * d for x, d in zip(kq, decays)]
    t_invs = _unit_lower_inverses(a_mats, xor)
    sols = [_mm(t, jnp.concatenate([v * b, kb * e], axis=-1))
            for t, v, b, kb, e in zip(t_invs, vs, betas, k_betas, e_gs)]
    states = [state_ref[s, h] for s, h in pairs]
    wss = [_mm(jnp.concatenate([sol[:, A_DV:], q * e], axis=0), st)
           for sol, q, e, st in zip(sols, qs, e_gs, states)]
    v_news = [(sol[:, :A_DV] - ws[:c]).astype(BF16) for sol, ws in zip(sols, wss)]
    outs = [ws[c:] + _mm(qk, vn) for ws, qk, vn in zip(wss, qks, v_news)]
    for i, (s, h) in enumerate(pairs):
        gl = A_HEADS + h
        k_dec = ks[i] * gates[s]["ekd"][:, gl:gl + 1]
        state_ref[s, h] = states[i] * gates[s]["elast"][:, gl:gl + 1] + lax.dot_general(
            k_dec.astype(BF16), v_news[i], (((0,), (0,)), ((), ())), preferred_element_type=F32)
    for i, (s, h) in enumerate(pairs):
        gate = main_ref[s, :, A_QKV + h * A_DV:A_QKV + (h + 1) * A_DV]
        o_ref[s, :, h * A_DV:(h + 1) * A_DV] = (_rms(outs[i], onorm_ref[...]) * _silu(gate)).astype(BF16)


def _delta_mixer_core(main, ba, a_log, dt_bias, o_norm, batch, seq):
    n_chunks = seq // CHUNK
    lane_pad = lambda p, lo: jnp.pad(p, (lo, LANES - lo - A_HEADS)).reshape(1, LANES)
    rows = lambda b, n: (b, n, 0)
    per_seq = lambda a: a.reshape(batch, seq, a.shape[-1])
    out = pl.pallas_call(
        _delta_body,
        grid=(batch // DELTA_SEQS, n_chunks),
        in_specs=[pl.BlockSpec((DELTA_SEQS, CHUNK, A_MAIN), rows), pl.BlockSpec((DELTA_SEQS, CHUNK, LANES), rows),
                  _const_spec((1, LANES)), _const_spec((1, LANES)), _const_spec((1, A_DV))],
        out_specs=pl.BlockSpec((DELTA_SEQS, CHUNK, A_HEADS * A_DV), rows),
        out_shape=jax.ShapeDtypeStruct((batch, seq, A_HEADS * A_DV), BF16),
        scratch_shapes=[pltpu.VMEM((DELTA_SEQS, A_HEADS, A_DK, A_DV), F32)],
        compiler_params=_params("parallel", "arbitrary"),
        name="delta_rule",
    )(per_seq(main), per_seq(ba), lane_pad(a_log, A_HEADS), lane_pad(dt_bias, A_HEADS), o_norm.reshape(1, A_DV))
    return out.reshape(batch * seq, A_HEADS * A_DV)


def _swa_body(sink_ref, x_ref, g_ref, wq_ref, bq_ref, kvp_ref, kvc_ref, wo_ref, y_ref, q_ref, o_ref):
    first_tile = pl.program_id(1) == 0
    kv_half = B_KV_HEADS * B_DH
    x = x_ref[...]
    h = _rms(x, g_ref[...]).astype(BF16)
    q_ref[...] = ((jnp.dot(h, wq_ref[...], preferred_element_type=F32) + bq_ref[...]) * (B_DH ** -0.5)).astype(BF16)

    qi = lax.broadcasted_iota(jnp.int32, (WINDOW, 2 * WINDOW), 0)
    ki = lax.broadcasted_iota(jnp.int32, (WINDOW, 2 * WINDOW), 1)
    in_win = (ki > qi) & (ki <= qi + WINDOW)
    bias_inner = jnp.where(in_win, 0.0, -jnp.inf)
    bias_first = jnp.where(in_win & (jnp.logical_not(first_tile) | (ki >= WINDOW)), 0.0, -jnp.inf)
    items = [(sub, hq) for sub in range(SWA_ROWS // WINDOW) for hq in range(B_Q_HEADS)]
    keys = [jnp.concatenate([kvp_ref[...], kvc_ref[0:WINDOW, :]], axis=0)]
    keys += [kvc_ref[(sub - 1) * WINDOW:(sub + 1) * WINDOW, :] for sub in range(1, SWA_ROWS // WINDOW)]

    pad = jnp.zeros((2 * WINDOW, B_DH), BF16)
    one = jnp.ones((2 * WINDOW, B_DH), BF16)
    v_ext = [[jnp.concatenate([k[:, kv_half + hk * B_DH:kv_half + (hk + 1) * B_DH], pad, one, pad], axis=1)
              for hk in range(B_KV_HEADS)] for k in keys]

    def scores_of(item):
        sub, hq = item
        hk = hq // B_GROUP
        s = lax.dot_general(q_ref[sub * WINDOW:(sub + 1) * WINDOW, hq * B_DH:(hq + 1) * B_DH],
                            keys[sub][:, hk * B_DH:(hk + 1) * B_DH], (((1,), (1,)), ((), ())),
                            preferred_element_type=F32)
        return s + (bias_first if sub == 0 else bias_inner)

    for g0 in range(0, len(items), SWA_GROUP):
        group = items[g0:g0 + SWA_GROUP]
        scores = [scores_of(it) for it in group]
        sinks = [sink_ref[hq] for _, hq in group]
        maxes = [jnp.maximum(jnp.max(s, axis=-1, keepdims=True), sk) for s, sk in zip(scores, sinks)]
        probs = [jnp.exp(s - m) for s, m in zip(scores, maxes)]
        outs = [jnp.dot(p.astype(BF16), v_ext[sub][hq // B_GROUP], preferred_element_type=F32)
                for (sub, hq), p in zip(group, probs)]
        for (sub, hq), o, sk, m in zip(group, outs, sinks, maxes):
            denom = o[:, 2 * B_DH:3 * B_DH] + jnp.exp(sk - m)
            o_ref[sub * WINDOW:(sub + 1) * WINDOW, hq * B_DH:(hq + 1) * B_DH] = (o[:, :B_DH] / denom).astype(BF16)
    y_ref[...] = x + jnp.dot(o_ref[...], wo_ref[...], preferred_element_type=F32)


def _swa_mixer(x, g, w_q, b_q, sinks, w_out, kv, batch, seq):
    tiles = seq // SWA_ROWS
    per_tile = SWA_ROWS // WINDOW
    dq = B_Q_HEADS * B_DH
    row = pl.BlockSpec((SWA_ROWS, D_MODEL), lambda b, i: (b * tiles + i, 0))
    prev = pl.BlockSpec((WINDOW, kv.shape[1]),
                        lambda b, i: ((b * tiles + i) * per_tile - jnp.minimum(i, 1), 0))
    cur = pl.BlockSpec((SWA_ROWS, kv.shape[1]), lambda b, i: (b * tiles + i, 0))
    return pl.pallas_call(
        _swa_body,
        grid=(batch, tiles),
        in_specs=[pl.BlockSpec(memory_space=pltpu.SMEM), row, _const_spec((1, D_MODEL)),
                  _const_spec((D_MODEL, dq)), _const_spec((1, dq)), prev, cur, _const_spec((dq, D_MODEL))],
        out_specs=row,
        out_shape=jax.ShapeDtypeStruct((batch * seq, D_MODEL), F32),
        scratch_shapes=[pltpu.VMEM((SWA_ROWS, dq), BF16), pltpu.VMEM((SWA_ROWS, dq), BF16)],
        compiler_params=_params("parallel", "arbitrary"),
        name="swa_mixer",
    )(sinks, x, g.reshape(1, D_MODEL), w_q.astype(BF16), b_q.reshape(1, dq), kv, kv, w_out.astype(BF16))


def kernel(x, ffn_norm, ffn_w_gate, ffn_w_up, ffn_w_down, mix_norm, a_w_in, a_conv, a_A_log, a_dt_bias,
           a_o_norm, a_w_out, kv_norm, kv_w, kv_b, b_w_q, b_b_q, b_sinks, b_w_out, final_norm):
    batch, seq, d = x.shape
    assert d == D_MODEL and ffn_w_gate.shape == (DEPTH, 2, D_MODEL, D_FF) and a_w_in.shape[0] == N_A_LAYERS
    assert a_w_in.shape[2] == A_MAIN + 2 * A_HEADS and kv_w.shape == (D_MODEL, 2 * B_KV_HEADS * B_DH)
    assert batch % DELTA_SEQS == 0 and seq % max(PROJ_ROWS, SWA_ROWS, CHUNK) == 0
    assert (batch * seq) % FFN_ROWS == 0
    x = x.reshape(batch * seq, d)
    wg, wu, wd = ffn_w_gate.astype(BF16), ffn_w_up.astype(BF16), ffn_w_down.astype(BF16)
    ffn_w = lambda i, s: (ffn_norm[i, s], wg, wu, wd, (i, s))
    kv = None
    for i in range(DEPTH):
        x = _ffn(x, *ffn_w(i, 0))
        if i < N_A_LAYERS:
            main, ba = _a_in_proj(x, mix_norm[i], a_w_in[i], a_conv[i], seq)
            o = _delta_mixer_core(main, ba, a_A_log[i], a_dt_bias[i], a_o_norm[i], batch, seq)
            if i == N_A_LAYERS - 1:
                x, kv = _ffn(x, *ffn_w(i, 1), pre=(o, a_w_out[i]), post="kv", post_args=(kv_norm, kv_w, kv_b))
            else:
                x = _ffn(x, *ffn_w(i, 1), pre=(o, a_w_out[i]))
        else:
            j = i - N_A_LAYERS
            x = _swa_mixer(x, mix_norm[i], b_w_q[j], b_b_q[j], b_sinks[j], b_w_out[j], kv, batch, seq)
            if i == DEPTH - 1:
                x = _ffn(x, *ffn_w(i, 1), post="final", post_args=(final_norm,))
            else:
                x = _ffn(x, *ffn_w(i, 1))
    return x.reshape(batch, seq, d)
```

```python
import functools

import jax
import jax.numpy as jnp
from jax import lax
from jax.experimental import pallas as pl
from jax.experimental.pallas import tpu as pltpu

F32 = jnp.float32
BF16 = jnp.bfloat16
HIGHEST = lax.Precision.HIGHEST

D_MODEL = 1024
D_FF = 2816
DEPTH = 4
N_A_LAYERS = 2
A_HEADS = 8
A_DK = 128
A_DV = 128
CONV_K = 4
CHUNK = 128
DELTA_SEQS = 4
INV_BASE = 8
A_QKV = A_HEADS * (2 * A_DK + A_DV)
A_MAIN = A_QKV + A_HEADS * A_DV
B_Q_HEADS = 16
B_KV_HEADS = 4
B_GROUP = B_Q_HEADS // B_KV_HEADS
B_DH = 64
WINDOW = 128
EPS = 1e-6

LANES = 128
CONV_HALO = 8
VMEM_LIMIT = 56 * 1024 * 1024

FFN_ROWS = 512
FFN_CHUNK = 256
PROJ_ROWS = 512
SWA_ROWS = 512
SWA_GROUP = 8


def _rms(x, g):
    return x * lax.rsqrt(jnp.mean(x * x, axis=-1, keepdims=True) + EPS) * g


def _silu(x):
    return x / (1.0 + jnp.exp(-x))


def _const_spec(shape):
    return pl.BlockSpec(shape, lambda *_: (0,) * len(shape), pipeline_mode=pl.Buffered(1))


def _stack_spec(shape, which):
    return pl.BlockSpec((None, None) + tuple(shape[2:]), lambda *_: tuple(which) + (0, 0),
                        pipeline_mode=pl.Buffered(1))


def _params(*semantics):
    return pltpu.CompilerParams(dimension_semantics=semantics, vmem_limit_bytes=VMEM_LIMIT)


def _swiglu_residual(x, g_ref, wg_ref, wu_ref, wd_ref):
    h = _rms(x, g_ref[...]).astype(BF16)
    acc = x
    for c0 in range(0, D_FF, FFN_CHUNK):
        cols = slice(c0, c0 + FFN_CHUNK)
        gate = jnp.dot(h, wg_ref[:, cols].astype(BF16), preferred_element_type=F32)
        up = jnp.dot(h, wu_ref[:, cols].astype(BF16), preferred_element_type=F32)
        act = (_silu(gate) * (0.5 * up)).astype(BF16)
        acc = acc + jnp.dot(act, wd_ref[cols, :].astype(BF16), preferred_element_type=F32)
    return acc


def _ffn_body(*refs, pre, post):
    it = iter(refs)
    x_ref = next(it)
    if pre:
        o_ref, wo_ref = next(it), next(it)
    g_ref, wg_ref, wu_ref, wd_ref = next(it), next(it), next(it), next(it)
    if post == "final":
        fg_ref = next(it)
    elif post == "kv":
        kg_ref, kw_ref, kb_ref = next(it), next(it), next(it)
    y_ref = next(it)

    x = x_ref[...]
    if pre:
        x = x + jnp.dot(o_ref[...], wo_ref[...], preferred_element_type=F32)
    y = _swiglu_residual(x, g_ref, wg_ref, wu_ref, wd_ref)
    if post == "final":
        y = _rms(y, fg_ref[...])
    y_ref[...] = y
    if post == "kv":
        kv_ref = next(it)
        hk = _rms(y, kg_ref[...]).astype(BF16)
        kv_ref[...] = (jnp.dot(hk, kw_ref[...], preferred_element_type=F32) + kb_ref[...]).astype(BF16)


def _ffn(x, g, wg, wu, wd, which, pre=None, post=None, post_args=()):
    t = x.shape[0]
    row = pl.BlockSpec((FFN_ROWS, D_MODEL), lambda i: (i, 0))
    vec = lambda v: v.reshape(1, v.shape[-1])
    args, specs = [x], [row]
    if pre is not None:
        o, w_out = pre
        args += [o, w_out.astype(BF16)]
        specs += [pl.BlockSpec((FFN_ROWS, o.shape[1]), lambda i: (i, 0)), _const_spec(w_out.shape)]
    args += [vec(g), wg, wu, wd]
    specs += [_const_spec((1, D_MODEL)), _stack_spec(wg.shape, which), _stack_spec(wu.shape, which),
              _stack_spec(wd.shape, which)]
    out_shape = [jax.ShapeDtypeStruct((t, D_MODEL), F32)]
    out_specs = [row]
    if post == "final":
        args += [vec(post_args[0])]
        specs += [_const_spec((1, D_MODEL))]
    elif post == "kv":
        kg, kw, kb = post_args
        n = kw.shape[1]
        args += [vec(kg), kw.astype(BF16), vec(kb)]
        specs += [_const_spec((1, D_MODEL)), _const_spec((D_MODEL, n)), _const_spec((1, n))]
        out_shape.append(jax.ShapeDtypeStruct((t, n), BF16))
        out_specs.append(pl.BlockSpec((FFN_ROWS, n), lambda i: (i, 0)))
    outs = pl.pallas_call(
        functools.partial(_ffn_body, pre=pre is not None, post=post),
        grid=(t // FFN_ROWS,),
        in_specs=specs,
        out_specs=out_specs,
        out_shape=out_shape,
        compiler_params=_params("parallel"),
        name="ffn" + ("_pre" if pre is not None else "") + ("_" + post if post else ""),
    )(*args)
    return outs[0] if len(outs) == 1 else outs


def _a_in_body(x_ref, g_ref, wm_ref, wb_ref, cw_ref, main_ref, ba_ref, hist_ref, *, tiles_per_seq):
    r = PROJ_ROWS

    @pl.when(pl.program_id(0) % tiles_per_seq == 0)
    def _():
        hist_ref[0:CONV_HALO, :] = jnp.zeros((CONV_HALO, A_QKV), F32)

    h = _rms(x_ref[...], g_ref[...]).astype(BF16)
    ba_ref[...] = jnp.dot(h, wb_ref[...], preferred_element_type=F32)
    main_ref[:, A_QKV:] = jnp.dot(h, wm_ref[:, A_QKV:], preferred_element_type=F32)
    hist_ref[CONV_HALO:CONV_HALO + r, :] = jnp.dot(h, wm_ref[:, :A_QKV], preferred_element_type=F32)

    for blk in range(A_QKV // LANES):
        lanes = slice(blk * LANES, (blk + 1) * LANES)
        conv = cw_ref[CONV_K - 1:CONV_K, lanes] * hist_ref[CONV_HALO:CONV_HALO + r, lanes]
        for j in range(CONV_K - 1):
            lag = CONV_K - 1 - j
            conv = conv + cw_ref[j:j + 1, lanes] * hist_ref[CONV_HALO - lag:CONV_HALO - lag + r, lanes]
        y = _silu(conv)
        if blk < 2 * A_HEADS:
            y = y * lax.rsqrt(jnp.sum(y * y, axis=-1, keepdims=True) + EPS)
        if blk < A_HEADS:
            y = y * (A_DK ** -0.5)
        main_ref[:, lanes] = y
    hist_ref[0:CONV_HALO, :] = hist_ref[r:r + CONV_HALO, :]


def _a_in_proj(x, g, w_in, conv_w, seq):
    t = x.shape[0]
    w_main = w_in[:, :A_MAIN].astype(BF16)
    w_ba = jnp.pad(w_in[:, A_MAIN:], ((0, 0), (0, LANES - 2 * A_HEADS))).astype(BF16)
    return pl.pallas_call(
        functools.partial(_a_in_body, tiles_per_seq=seq // PROJ_ROWS),
        grid=(t // PROJ_ROWS,),
        in_specs=[pl.BlockSpec((PROJ_ROWS, D_MODEL), lambda i: (i, 0)), _const_spec((1, D_MODEL)),
                  _const_spec((D_MODEL, A_MAIN)), _const_spec((D_MODEL, LANES)), _const_spec((CONV_K, A_QKV))],
        out_specs=[pl.BlockSpec((PROJ_ROWS, A_MAIN), lambda i: (i, 0)),
                   pl.BlockSpec((PROJ_ROWS, LANES), lambda i: (i, 0))],
        out_shape=[jax.ShapeDtypeStruct((t, A_MAIN), F32), jax.ShapeDtypeStruct((t, LANES), F32)],
        scratch_shapes=[pltpu.VMEM((PROJ_ROWS + CONV_HALO, A_QKV), F32)],
        compiler_params=_params("arbitrary"),
        name="a_in_proj",
    )(x, g.reshape(1, D_MODEL), w_main, w_ba, conv_w)


def _mm(a, b):
    return jnp.dot(a.astype(BF16), b.astype(BF16), preferred_element_type=F32)


def _mm_nt(a, b):
    return lax.dot_general(a.astype(BF16), b.astype(BF16), (((1,), (1,)), ((), ())), preferred_element_type=F32)


def _unit_lower_inverses(mats, xor):
    c = mats[0].shape[0]
    ps = [jnp.where(xor < INV_BASE, -a, 0.0) for a in mats]
    ts = [jnp.where(xor == 0, 1.0, 0.0) + p for p in ps]
    k = 2
    while k < INV_BASE:
        ps = [_mm(p, p) for p in ps]
        ts = [t + _mm(t, p) for t, p in zip(ts, ps)]
        k *= 2
    s = INV_BASE
    while s < c:
        off = (xor >= s) & (xor < 2 * s)
        ats = [_mm(jnp.where(off, a, 0.0), t) for a, t in zip(mats, ts)]
        ts = [t - _mm(t, at) for t, at in zip(ts, ats)]
        s *= 2
    return ts


def _delta_body(main_ref, ba_ref, alog_ref, dtb_ref, onorm_ref, o_ref, state_ref):
    c = CHUNK

    @pl.when(pl.program_id(1) == 0)
    def _():
        state_ref[...] = jnp.zeros(state_ref.shape, F32)

    row = lax.broadcasted_iota(jnp.int32, (c, c), 0)
    col = lax.broadcasted_iota(jnp.int32, (c, c), 1)
    lower = row >= col
    strict = row > col
    xor = row ^ col
    tri = jnp.where(lower, 1.0, 0.0)

    pairs = [(s, h) for s in range(DELTA_SEQS) for h in range(A_HEADS)]
    gates = []
    for s in range(DELTA_SEQS):
        ba = ba_ref[s]
        beta_all = 1.0 / (1.0 + jnp.exp(-ba))
        z = ba + dtb_ref[...]
        softplus = jnp.maximum(z, 0.0) + jnp.log(1.0 + jnp.exp(-jnp.abs(z)))
        g_all = -jnp.exp(alog_ref[...]) * softplus
        gcum_all = jnp.dot(tri, g_all, precision=HIGHEST, preferred_element_type=F32)
        glast_all = gcum_all[c - 1:c, :]
        gates.append(dict(beta=beta_all, gcum=gcum_all, gcum_t=gcum_all.T, eg=jnp.exp(gcum_all),
                          ekd=jnp.exp(glast_all - gcum_all), elast=jnp.exp(glast_all)))

    qs, ks, vs, betas, e_gs, decays = [], [], [], [], [], []
    for s, h in pairs:
        gl = A_HEADS + h
        gt = gates[s]
        qs.append(main_ref[s, :, h * A_DK:(h + 1) * A_DK])
        ks.append(main_ref[s, :, A_HEADS * A_DK + h * A_DK:A_HEADS * A_DK + (h + 1) * A_DK])
        vs.append(main_ref[s, :, 2 * A_HEADS * A_DK + h * A_DV:2 * A_HEADS * A_DK + (h + 1) * A_DV])
        betas.append(gt["beta"][:, h:h + 1])
        e_gs.append(gt["eg"][:, gl:gl + 1])
        decays.append(jnp.exp(jnp.where(lower, gt["gcum"][:, gl:gl + 1] - gt["gcum_t"][gl:gl + 1, :], -jnp.inf)))
    k_betas = [k * b for k, b in zip(ks, betas)]
    kq = [_mm_nt(jnp.concatenate([kb, q], axis=0), k) for kb, q, k in zip(k_betas, qs, ks)]
    a_mats = [jnp.where(strict, x[:c] * d, 0.0) for x, d in zip(kq, decays)]
    qks = [x[c:] * d for x, d in zip(kq, decays)]
    t_invs = _unit_lower_inverses(a_mats, xor)
    sols = [_mm(t, jnp.concatenate([v * b, kb * e], axis=-1))
            for t, v, b, kb, e in zip(t_invs, vs, betas, k_betas, e_gs)]
    states = [state_ref[s, h] for s, h in pairs]
    wss = [_mm(jnp.concatenate([sol[:, A_DV:], q * e], axis=0), st)
           for sol, q, e, st in zip(sols, qs, e_gs, states)]
    v_news = [(sol[:, :A_DV] - ws[:c]).astype(BF16) for sol, ws in zip(sols, wss)]
    outs = [ws[c:] + _mm(qk, vn) for ws, qk, vn in zip(wss, qks, v_news)]
    for i, (s, h) in enumerate(pairs):
        gl = A_HEADS + h
        k_dec = ks[i] * gates[s]["ekd"][:, gl:gl + 1]
        state_ref[s, h] = states[i] * gates[s]["elast"][:, gl:gl + 1] + lax.dot_general(
            k_dec.astype(BF16), v_news[i], (((0,), (0,)), ((), ())), preferred_element_type=F32)
    for i, (s, h) in enumerate(pairs):
        gate = main_ref[s, :, A_QKV + h * A_DV:A_QKV + (h + 1) * A_DV]
        o_ref[s, :, h * A_DV:(h + 1) * A_DV] = (_rms(outs[i], onorm_ref[...]) * _silu(gate)).astype(BF16)


def _delta_mixer_core(main, ba, a_log, dt_bias, o_norm, batch, seq):
    n_chunks = seq // CHUNK
    lane_pad = lambda p, lo: jnp.pad(p, (lo, LANES - lo - A_HEADS)).reshape(1, LANES)
    rows = lambda b, n: (b, n, 0)
    per_seq = lambda a: a.reshape(batch, seq, a.shape[-1])
    out = pl.pallas_call(
        _delta_body,
        grid=(batch // DELTA_SEQS, n_chunks),
        in_specs=[pl.BlockSpec((DELTA_SEQS, CHUNK, A_MAIN), rows), pl.BlockSpec((DELTA_SEQS, CHUNK, LANES), rows),
                  _const_spec((1, LANES)), _const_spec((1, LANES)), _const_spec((1, A_DV))],
        out_specs=pl.BlockSpec((DELTA_SEQS, CHUNK, A_HEADS * A_DV), rows),
        out_shape=jax.ShapeDtypeStruct((batch, seq, A_HEADS * A_DV), BF16),
        scratch_shapes=[pltpu.VMEM((DELTA_SEQS, A_HEADS, A_DK, A_DV), F32)],
        compiler_params=_params("parallel", "arbitrary"),
        name="delta_rule",
    )(per_seq(main), per_seq(ba), lane_pad(a_log, A_HEADS), lane_pad(dt_bias, A_HEADS), o_norm.reshape(1, A_DV))
    return out.reshape(batch * seq, A_HEADS * A_DV)


def _swa_body(sink_ref, x_ref, g_ref, wq_ref, bq_ref, kvp_ref, kvc_ref, wo_ref, y_ref, q_ref, o_ref):
    first_tile = pl.program_id(1) == 0
    kv_half = B_KV_HEADS * B_DH
    x = x_ref[...]
    h = _rms(x, g_ref[...]).astype(BF16)
    q_ref[...] = ((jnp.dot(h, wq_ref[...], preferred_element_type=F32) + bq_ref[...]) * (B_DH ** -0.5)).astype(BF16)

    qi = lax.broadcasted_iota(jnp.int32, (WINDOW, 2 * WINDOW), 0)
    ki = lax.broadcasted_iota(jnp.int32, (WINDOW, 2 * WINDOW), 1)
    in_win = (ki > qi) & (ki <= qi + WINDOW)
    bias_inner = jnp.where(in_win, 0.0, -jnp.inf)
    bias_first = jnp.where(in_win & (jnp.logical_not(first_tile) | (ki >= WINDOW)), 0.0, -jnp.inf)
    items = [(sub, hq) for sub in range(SWA_ROWS // WINDOW) for hq in range(B_Q_HEADS)]
    keys = [jnp.concatenate([kvp_ref[...], kvc_ref[0:WINDOW, :]], axis=0)]
    keys += [kvc_ref[(sub - 1) * WINDOW:(sub + 1) * WINDOW, :] for sub in range(1, SWA_ROWS // WINDOW)]

    pad = jnp.zeros((2 * WINDOW, B_DH), BF16)
    one = jnp.ones((2 * WINDOW, B_DH), BF16)
    v_ext = [[jnp.concatenate([k[:, kv_half + hk * B_DH:kv_half + (hk + 1) * B_DH], pad, one, pad], axis=1)
              for hk in range(B_KV_HEADS)] for k in keys]

    def scores_of(item):
        sub, hq = item
        hk = hq // B_GROUP
        s = lax.dot_general(q_ref[sub * WINDOW:(sub + 1) * WINDOW, hq * B_DH:(hq + 1) * B_DH],
                            keys[sub][:, hk * B_DH:(hk + 1) * B_DH], (((1,), (1,)), ((), ())),
                            preferred_element_type=F32)
        return s + (bias_first if sub == 0 else bias_inner)

    for g0 in range(0, len(items), SWA_GROUP):
        group = items[g0:g0 + SWA_GROUP]
        scores = [scores_of(it) for it in group]
        sinks = [sink_ref[hq] for _, hq in group]
        maxes = [jnp.maximum(jnp.max(s, axis=-1, keepdims=True), sk) for s, sk in zip(scores, sinks)]
        probs = [jnp.exp(s - m) for s, m in zip(scores, maxes)]
        outs = [jnp.dot(p.astype(BF16), v_ext[sub][hq // B_GROUP], preferred_element_type=F32)
                for (sub, hq), p in zip(group, probs)]
        for (sub, hq), o, sk, m in zip(group, outs, sinks, maxes):
            denom = o[:, 2 * B_DH:3 * B_DH] + jnp.exp(sk - m)
            o_ref[sub * WINDOW:(sub + 1) * WINDOW, hq * B_DH:(hq + 1) * B_DH] = (o[:, :B_DH] / denom).astype(BF16)
    y_ref[...] = x + jnp.dot(o_ref[...], wo_ref[...], preferred_element_type=F32)


def _swa_mixer(x, g, w_q, b_q, sinks, w_out, kv, batch, seq):
    tiles = seq // SWA_ROWS
    per_tile = SWA_ROWS // WINDOW
    dq = B_Q_HEADS * B_DH
    row = pl.BlockSpec((SWA_ROWS, D_MODEL), lambda b, i: (b * tiles + i, 0))
    prev = pl.BlockSpec((WINDOW, kv.shape[1]),
                        lambda b, i: ((b * tiles + i) * per_tile - jnp.minimum(i, 1), 0))
    cur = pl.BlockSpec((SWA_ROWS, kv.shape[1]), lambda b, i: (b * tiles + i, 0))
    return pl.pallas_call(
        _swa_body,
        grid=(batch, tiles),
        in_specs=[pl.BlockSpec(memory_space=pltpu.SMEM), row, _const_spec((1, D_MODEL)),
                  _const_spec((D_MODEL, dq)), _const_spec((1, dq)), prev, cur, _const_spec((dq, D_MODEL))],
        out_specs=row,
        out_shape=jax.ShapeDtypeStruct((batch * seq, D_MODEL), F32),
        scratch_shapes=[pltpu.VMEM((SWA_ROWS, dq), BF16), pltpu.VMEM((SWA_ROWS, dq), BF16)],
        compiler_params=_params("parallel", "arbitrary"),
        name="swa_mixer",
    )(sinks, x, g.reshape(1, D_MODEL), w_q.astype(BF16), b_q.reshape(1, dq), kv, kv, w_out.astype(BF16))


def kernel(x, ffn_norm, ffn_w_gate, ffn_w_up, ffn_w_down, mix_norm, a_w_in, a_conv, a_A_log, a_dt_bias,
           a_o_norm, a_w_out, kv_norm, kv_w, kv_b, b_w_q, b_b_q, b_sinks, b_w_out, final_norm):
    batch, seq, d = x.shape
    assert d == D_MODEL and ffn_w_gate.shape == (DEPTH, 2, D_MODEL, D_FF) and a_w_in.shape[0] == N_A_LAYERS
    assert a_w_in.shape[2] == A_MAIN + 2 * A_HEADS and kv_w.shape == (D_MODEL, 2 * B_KV_HEADS * B_DH)
    assert batch % DELTA_SEQS == 0 and seq % max(PROJ_ROWS, SWA_ROWS, CHUNK) == 0
    assert (batch * seq) % FFN_ROWS == 0
    x = x.reshape(batch * seq, d)
    ffn_w = lambda i, s: (ffn_norm[i, s], ffn_w_gate, ffn_w_up, ffn_w_down, (i, s))
    kv = None
    for i in range(DEPTH):
        x = _ffn(x, *ffn_w(i, 0))
        if i < N_A_LAYERS:
            main, ba = _a_in_proj(x, mix_norm[i], a_w_in[i], a_conv[i], seq)
            o = _delta_mixer_core(main, ba, a_A_log[i], a_dt_bias[i], a_o_norm[i], batch, seq)
            if i == N_A_LAYERS - 1:
                x, kv = _ffn(x, *ffn_w(i, 1), pre=(o, a_w_out[i]), post="kv", post_args=(kv_norm, kv_w, kv_b))
            else:
                x = _ffn(x, *ffn_w(i, 1), pre=(o, a_w_out[i]))
        else:
            j = i - N_A_LAYERS
            x = _swa_mixer(x, mix_norm[i], b_w_q[j], b_b_q[j], b_sinks[j], b_w_out[j], kv, batch, seq)
            if i == DEPTH - 1:
                x = _ffn(x, *ffn_w(i, 1), post="final", post_args=(final_norm,))
            else:
                x = _ffn(x, *ffn_w(i, 1))
    return x.reshape(batch, seq, d)
```

```python
import functools

import jax
import jax.numpy as jnp
from jax import lax
from jax.experimental import pallas as pl
from jax.experimental.pallas import tpu as pltpu

F32 = jnp.float32
BF16 = jnp.bfloat16
HIGHEST = lax.Precision.HIGHEST

D_MODEL = 1024
D_FF = 2816
DEPTH = 4
N_A_LAYERS = 2
A_HEADS = 8
A_DK = 128
A_DV = 128
CONV_K = 4
CHUNK = 128
DELTA_SEQS = 4
INV_BASE = 8
A_QKV = A_HEADS * (2 * A_DK + A_DV)
A_MAIN = A_QKV + A_HEADS * A_DV
B_Q_HEADS = 16
B_KV_HEADS = 4
B_GROUP = B_Q_HEADS // B_KV_HEADS
B_DH = 64
WINDOW = 128
EPS = 1e-6

LANES = 128
CONV_HALO = 8
VMEM_LIMIT = 56 * 1024 * 1024

FFN_ROWS = 512
FFN_CHUNK = 256
PROJ_ROWS = 512
A_GROUP = 512
A_GATE_PART = 256
SWA_ROWS = 512
SWA_GROUP = 16


def _rms(x, g):
    return x * lax.rsqrt(jnp.mean(x * x, axis=-1, keepdims=True) + EPS) * g


def _silu(x):
    return x / (1.0 + jnp.exp(-x))


def _const_spec(shape):
    return pl.BlockSpec(shape, lambda *_: (0,) * len(shape), pipeline_mode=pl.Buffered(1))


def _stack_spec(shape, which):
    return pl.BlockSpec((None, None) + tuple(shape[2:]), lambda *_: tuple(which) + (0, 0),
                        pipeline_mode=pl.Buffered(1))


def _params(*semantics):
    return pltpu.CompilerParams(dimension_semantics=semantics, vmem_limit_bytes=VMEM_LIMIT)


def _swiglu_residual(x, g_ref, wg_ref, wu_ref, wd_ref):
    h = _rms(x, g_ref[...]).astype(BF16)
    acc = x
    for c0 in range(0, D_FF, FFN_CHUNK):
        cols = slice(c0, c0 + FFN_CHUNK)
        gate = jnp.dot(h, wg_ref[:, cols].astype(BF16), preferred_element_type=F32)
        up = jnp.dot(h, wu_ref[:, cols].astype(BF16), preferred_element_type=F32)
        act = (_silu(gate) * (0.5 * up)).astype(BF16)
        acc = acc + jnp.dot(act, wd_ref[cols, :].astype(BF16), preferred_element_type=F32)
    return acc


def _ffn_body(*refs, pre, post):
    it = iter(refs)
    x_ref = next(it)
    if pre:
        o_ref, wo_ref = next(it), next(it)
    g_ref, wg_ref, wu_ref, wd_ref = next(it), next(it), next(it), next(it)
    if post == "final":
        fg_ref = next(it)
    elif post == "kv":
        kg_ref, kw_ref, kb_ref = next(it), next(it), next(it)
    y_ref = next(it)

    x = x_ref[...]
    if pre:
        x = x + jnp.dot(o_ref[...], wo_ref[...], preferred_element_type=F32)
    y = _swiglu_residual(x, g_ref, wg_ref, wu_ref, wd_ref)
    if post == "final":
        y = _rms(y, fg_ref[...])
    y_ref[...] = y
    if post == "kv":
        kv_ref = next(it)
        hk = _rms(y, kg_ref[...]).astype(BF16)
        kv_ref[...] = (jnp.dot(hk, kw_ref[...], preferred_element_type=F32) + kb_ref[...]).astype(BF16)


def _ffn(x, g, wg, wu, wd, which, pre=None, post=None, post_args=()):
    t = x.shape[0]
    row = pl.BlockSpec((FFN_ROWS, D_MODEL), lambda i: (i, 0))
    vec = lambda v: v.reshape(1, v.shape[-1])
    args, specs = [x], [row]
    if pre is not None:
        o, w_out = pre
        args += [o, w_out.astype(BF16)]
        specs += [pl.BlockSpec((FFN_ROWS, o.shape[1]), lambda i: (i, 0)), _const_spec(w_out.shape)]
    args += [vec(g), wg, wu, wd]
    specs += [_const_spec((1, D_MODEL)), _stack_spec(wg.shape, which), _stack_spec(wu.shape, which),
              _stack_spec(wd.shape, which)]
    out_shape = [jax.ShapeDtypeStruct((t, D_MODEL), F32)]
    out_specs = [row]
    if post == "final":
        args += [vec(post_args[0])]
        specs += [_const_spec((1, D_MODEL))]
    elif post == "kv":
        kg, kw, kb = post_args
        n = kw.shape[1]
        args += [vec(kg), kw.astype(BF16), vec(kb)]
        specs += [_const_spec((1, D_MODEL)), _const_spec((D_MODEL, n)), _const_spec((1, n))]
        out_shape.append(jax.ShapeDtypeStruct((t, n), BF16))
        out_specs.append(pl.BlockSpec((FFN_ROWS, n), lambda i: (i, 0)))
    outs = pl.pallas_call(
        functools.partial(_ffn_body, pre=pre is not None, post=post),
        grid=(t // FFN_ROWS,),
        in_specs=specs,
        out_specs=out_specs,
        out_shape=out_shape,
        compiler_params=_params("parallel"),
        name="ffn" + ("_pre" if pre is not None else "") + ("_" + post if post else ""),
    )(*args)
    return outs[0] if len(outs) == 1 else outs


def _a_in_body(x_ref, g_ref, wm_ref, wb_ref, cw_ref, main_ref, ba_ref, hist_ref, *, tiles_per_seq):
    r = PROJ_ROWS

    @pl.when(pl.program_id(0) % tiles_per_seq == 0)
    def _():
        hist_ref[0:CONV_HALO, :] = jnp.zeros((CONV_HALO, A_QKV), F32)

    h = _rms(x_ref[...], g_ref[...]).astype(BF16)

    def project(group):
        cols = slice(group * A_GROUP, (group + 1) * A_GROUP)
        hist_ref[CONV_HALO:CONV_HALO + r, cols] = jnp.dot(h, wm_ref[:, cols], preferred_element_type=F32)

    def gate_part(part):
        cols = slice(A_QKV + part * A_GATE_PART, A_QKV + (part + 1) * A_GATE_PART)
        main_ref[:, cols] = jnp.dot(h, wm_ref[:, cols], preferred_element_type=F32)

    def conv_block(blk):
        lanes = slice(blk * LANES, (blk + 1) * LANES)
        conv = cw_ref[CONV_K - 1:CONV_K, lanes] * hist_ref[CONV_HALO:CONV_HALO + r, lanes]
        for j in range(CONV_K - 1):
            lag = CONV_K - 1 - j
            conv = conv + cw_ref[j:j + 1, lanes] * hist_ref[CONV_HALO - lag:CONV_HALO - lag + r, lanes]
        y = _silu(conv)
        if blk < 2 * A_HEADS:
            y = y * lax.rsqrt(jnp.sum(y * y, axis=-1, keepdims=True) + EPS)
        if blk < A_HEADS:
            y = y * (A_DK ** -0.5)
        main_ref[:, lanes] = y

    n_groups = A_QKV // A_GROUP
    per_group = A_GROUP // LANES
    extra = [functools.partial(gate_part, p) for p in range((A_MAIN - A_QKV) // A_GATE_PART)]
    project(0)
    for group in range(n_groups):
        if group + 1 < n_groups:
            project(group + 1)
        if extra:
            extra.pop(0)()
        for blk in range(group * per_group, (group + 1) * per_group):
            conv_block(blk)
    for part in extra:
        part()
    ba_ref[...] = jnp.dot(h, wb_ref[...], preferred_element_type=F32)
    hist_ref[0:CONV_HALO, :] = hist_ref[r:r + CONV_HALO, :]


def _a_in_proj(x, g, w_in, conv_w, seq):
    t = x.shape[0]
    w_main = w_in[:, :A_MAIN].astype(BF16)
    w_ba = jnp.pad(w_in[:, A_MAIN:], ((0, 0), (0, LANES - 2 * A_HEADS))).astype(BF16)
    return pl.pallas_call(
        functools.partial(_a_in_body, tiles_per_seq=seq // PROJ_ROWS),
        grid=(t // PROJ_ROWS,),
        in_specs=[pl.BlockSpec((PROJ_ROWS, D_MODEL), lambda i: (i, 0)), _const_spec((1, D_MODEL)),
                  _const_spec((D_MODEL, A_MAIN)), _const_spec((D_MODEL, LANES)), _const_spec((CONV_K, A_QKV))],
        out_specs=[pl.BlockSpec((PROJ_ROWS, A_MAIN), lambda i: (i, 0)),
                   pl.BlockSpec((PROJ_ROWS, LANES), lambda i: (i, 0))],
        out_shape=[jax.ShapeDtypeStruct((t, A_MAIN), F32), jax.ShapeDtypeStruct((t, LANES), F32)],
        scratch_shapes=[pltpu.VMEM((PROJ_ROWS + CONV_HALO, A_QKV), F32)],
        compiler_params=_params("arbitrary"),
        name="a_in_proj",
    )(x, g.reshape(1, D_MODEL), w_main, w_ba, conv_w)


def _mm(a, b):
    return jnp.dot(a.astype(BF16), b.astype(BF16), preferred_element_type=F32)


def _mm_nt(a, b):
    return lax.dot_general(a.astype(BF16), b.astype(BF16), (((1,), (1,)), ((), ())), preferred_element_type=F32)


def _unit_lower_inverses(mats, xor):
    c = mats[0].shape[0]
    ps = [jnp.where(xor < INV_BASE, -a, 0.0) for a in mats]
    ts = [jnp.where(xor == 0, 1.0, 0.0) + p for p in ps]
    k = 2
    while k < INV_BASE:
        ps = [_mm(p, p) for p in ps]
        ts = [t + _mm(t, p) for t, p in zip(ts, ps)]
        k *= 2
    s = INV_BASE
    while s < c:
        off = (xor >= s) & (xor < 2 * s)
        ats = [_mm(jnp.where(off, a, 0.0), t) for a, t in zip(mats, ts)]
        ts = [t - _mm(t, at) for t, at in zip(ts, ats)]
        s *= 2
    return ts


def _delta_body(main_ref, ba_ref, alog_ref, dtb_ref, onorm_ref, o_ref, state_ref):
    c = CHUNK

    @pl.when(pl.program_id(1) == 0)
    def _():
        state_ref[...] = jnp.zeros(state_ref.shape, F32)

    row = lax.broadcasted_iota(jnp.int32, (c, c), 0)
    col = lax.broadcasted_iota(jnp.int32, (c, c), 1)
    lower = row >= col
    strict = row > col
    xor = row ^ col
    tri = jnp.where(lower, 1.0, 0.0)

    pairs = [(s, h) for s in range(DELTA_SEQS) for h in range(A_HEADS)]
    gates = []
    for s in range(DELTA_SEQS):
        ba = ba_ref[s]
        beta_all = 1.0 / (1.0 + jnp.exp(-ba))
        z = ba + dtb_ref[...]
        softplus = jnp.maximum(z, 0.0) + jnp.log(1.0 + jnp.exp(-jnp.abs(z)))
        g_all = -jnp.exp(alog_ref[...]) * softplus
        gcum_all = jnp.dot(tri, g_all, precision=HIGHEST, preferred_element_type=F32)
        glast_all = gcum_all[c - 1:c, :]
        gates.append(dict(beta=beta_all, gcum=gcum_all, gcum_t=gcum_all.T, eg=jnp.exp(gcum_all),
                          ekd=jnp.exp(glast_all - gcum_all), elast=jnp.exp(glast_all)))

    qs, ks, vs, betas, e_gs, decays = [], [], [], [], [], []
    for s, h in pairs:
        gl = A_HEADS + h
        gt = gates[s]
        qs.append(main_ref[s, :, h * A_DK:(h + 1) * A_DK])
        ks.append(main_ref[s, :, A_HEADS * A_DK + h * A_DK:A_HEADS * A_DK + (h + 1) * A_DK])
        vs.append(main_ref[s, :, 2 * A_HEADS * A_DK + h * A_DV:2 * A_HEADS * A_DK + (h + 1) * A_DV])
        betas.append(gt["beta"][:, h:h + 1])
        e_gs.append(gt["eg"][:, gl:gl + 1])
        decays.append(jnp.exp(jnp.where(lower, gt["gcum"][:, gl:gl + 1] - gt["gcum_t"][gl:gl + 1, :], -jnp.inf)))
    k_betas = [k * b for k, b in zip(ks, betas)]
    kq = [_mm_nt(jnp.concatenate([kb, q], axis=0), k) for kb, q, k in zip(k_betas, qs, ks)]
    a_mats = [jnp.where(strict, x[:c] * d, 0.0) for x, d in zip(kq, decays)]
    qks = [x[c:] * d for x, d in zip(kq, decays)]
    t_invs = _unit_lower_inverses(a_mats, xor)
    sols = [_mm(t, jnp.concatenate([v * b, kb * e], axis=-1))
            for t, v, b, kb, e in zip(t_invs, vs, betas, k_betas, e_gs)]
    states = [state_ref[s, h] for s, h in pairs]
    wss = [_mm(jnp.concatenate([sol[:, A_DV:], q * e], axis=0), st)
           for sol, q, e, st in zip(sols, qs, e_gs, states)]
    v_news = [(sol[:, :A_DV] - ws[:c]).astype(BF16) for sol, ws in zip(sols, wss)]
    outs = [ws[c:] + _mm(qk, vn) for ws, qk, vn in zip(wss, qks, v_news)]
    for i, (s, h) in enumerate(pairs):
        gl = A_HEADS + h
        k_dec = ks[i] * gates[s]["ekd"][:, gl:gl + 1]
        state_ref[s, h] = states[i] * gates[s]["elast"][:, gl:gl + 1] + lax.dot_general(
            k_dec.astype(BF16), v_news[i], (((0,), (0,)), ((), ())), preferred_element_type=F32)
    for i, (s, h) in enumerate(pairs):
        gate = main_ref[s, :, A_QKV + h * A_DV:A_QKV + (h + 1) * A_DV]
        o_ref[s, :, h * A_DV:(h + 1) * A_DV] = (_rms(outs[i], onorm_ref[...]) * _silu(gate)).astype(BF16)


def _delta_mixer_core(main, ba, a_log, dt_bias, o_norm, batch, seq):
    n_chunks = seq // CHUNK
    lane_pad = lambda p, lo: jnp.pad(p, (lo, LANES - lo - A_HEADS)).reshape(1, LANES)
    rows = lambda b, n: (b, n, 0)
    per_seq = lambda a: a.reshape(batch, seq, a.shape[-1])
    out = pl.pallas_call(
        _delta_body,
        grid=(batch // DELTA_SEQS, n_chunks),
        in_specs=[pl.BlockSpec((DELTA_SEQS, CHUNK, A_MAIN), rows), pl.BlockSpec((DELTA_SEQS, CHUNK, LANES), rows),
                  _const_spec((1, LANES)), _const_spec((1, LANES)), _const_spec((1, A_DV))],
        out_specs=pl.BlockSpec((DELTA_SEQS, CHUNK, A_HEADS * A_DV), rows),
        out_shape=jax.ShapeDtypeStruct((batch, seq, A_HEADS * A_DV), BF16),
        scratch_shapes=[pltpu.VMEM((DELTA_SEQS, A_HEADS, A_DK, A_DV), F32)],
        compiler_params=_params("parallel", "arbitrary"),
        name="delta_rule",
    )(per_seq(main), per_seq(ba), lane_pad(a_log, A_HEADS), lane_pad(dt_bias, A_HEADS), o_norm.reshape(1, A_DV))
    return out.reshape(batch * seq, A_HEADS * A_DV)


def _swa_body(sink_ref, x_ref, g_ref, wq_ref, bq_ref, kvp_ref, kvc_ref, wo_ref, y_ref, q_ref, o_ref):
    first_tile = pl.program_id(1) == 0
    kv_half = B_KV_HEADS * B_DH
    x = x_ref[...]
    h = _rms(x, g_ref[...]).astype(BF16)
    q_ref[...] = ((jnp.dot(h, wq_ref[...], preferred_element_type=F32) + bq_ref[...]) * (B_DH ** -0.5)).astype(BF16)

    qi = lax.broadcasted_iota(jnp.int32, (WINDOW, 2 * WINDOW), 0)
    ki = lax.broadcasted_iota(jnp.int32, (WINDOW, 2 * WINDOW), 1)
    in_win = (ki > qi) & (ki <= qi + WINDOW)
    bias_inner = jnp.where(in_win, 0.0, -jnp.inf)
    bias_first = jnp.where(in_win & (jnp.logical_not(first_tile) | (ki >= WINDOW)), 0.0, -jnp.inf)
    items = [(sub, hq) for sub in range(SWA_ROWS // WINDOW) for hq in range(B_Q_HEADS)]
    keys = [jnp.concatenate([kvp_ref[...], kvc_ref[0:WINDOW, :]], axis=0)]
    keys += [kvc_ref[(sub - 1) * WINDOW:(sub + 1) * WINDOW, :] for sub in range(1, SWA_ROWS // WINDOW)]

    pad = jnp.zeros((2 * WINDOW, B_DH), BF16)
    one = jnp.ones((2 * WINDOW, B_DH), BF16)
    v_ext = [[jnp.concatenate([k[:, kv_half + hk * B_DH:kv_half + (hk + 1) * B_DH], pad, one, pad], axis=1)
              for hk in range(B_KV_HEADS)] for k in keys]

    def scores_of(item):
        sub, hq = item
        hk = hq // B_GROUP
        s = lax.dot_general(q_ref[sub * WINDOW:(sub + 1) * WINDOW, hq * B_DH:(hq + 1) * B_DH],
                            keys[sub][:, hk * B_DH:(hk + 1) * B_DH], (((1,), (1,)), ((), ())),
                            preferred_element_type=F32)
        return s + (bias_first if sub == 0 else bias_inner)

    for g0 in range(0, len(items), SWA_GROUP):
        group = items[g0:g0 + SWA_GROUP]
        scores = [scores_of(it) for it in group]
        sinks = [sink_ref[hq] for _, hq in group]
        maxes = [jnp.maximum(jnp.max(s, axis=-1, keepdims=True), sk) for s, sk in zip(scores, sinks)]
        probs = [jnp.exp(s - m) for s, m in zip(scores, maxes)]
        outs = [jnp.dot(p.astype(BF16), v_ext[sub][hq // B_GROUP], preferred_element_type=F32)
                for (sub, hq), p in zip(group, probs)]
        for (sub, hq), o, sk, m in zip(group, outs, sinks, maxes):
            denom = o[:, 2 * B_DH:3 * B_DH] + jnp.exp(sk - m)
            o_ref[sub * WINDOW:(sub + 1) * WINDOW, hq * B_DH:(hq + 1) * B_DH] = (o[:, :B_DH] / denom).astype(BF16)
    y_ref[...] = x + jnp.dot(o_ref[...], wo_ref[...], preferred_element_type=F32)


def _swa_mixer(x, g, w_q, b_q, sinks, w_out, kv, batch, seq):
    tiles = seq // SWA_ROWS
    per_tile = SWA_ROWS // WINDOW
    dq = B_Q_HEADS * B_DH
    row = pl.BlockSpec((SWA_ROWS, D_MODEL), lambda b, i: (b * tiles + i, 0))
    prev = pl.BlockSpec((WINDOW, kv.shape[1]),
                        lambda b, i: ((b * tiles + i) * per_tile - jnp.minimum(i, 1), 0))
    cur = pl.BlockSpec((SWA_ROWS, kv.shape[1]), lambda b, i: (b * tiles + i, 0))
    return pl.pallas_call(
        _swa_body,
        grid=(batch, tiles),
        in_specs=[pl.BlockSpec(memory_space=pltpu.SMEM), row, _const_spec((1, D_MODEL)),
                  _const_spec((D_MODEL, dq)), _const_spec((1, dq)), prev, cur, _const_spec((dq, D_MODEL))],
        out_specs=row,
        out_shape=jax.ShapeDtypeStruct((batch * seq, D_MODEL), F32),
        scratch_shapes=[pltpu.VMEM((SWA_ROWS, dq), BF16), pltpu.VMEM((SWA_ROWS, dq), BF16)],
        compiler_params=_params("parallel", "arbitrary"),
        name="swa_mixer",
    )(sinks, x, g.reshape(1, D_MODEL), w_q.astype(BF16), b_q.reshape(1, dq), kv, kv, w_out.astype(BF16))


def kernel(x, ffn_norm, ffn_w_gate, ffn_w_up, ffn_w_down, mix_norm, a_w_in, a_conv, a_A_log, a_dt_bias,
           a_o_norm, a_w_out, kv_norm, kv_w, kv_b, b_w_q, b_b_q, b_sinks, b_w_out, final_norm):
    batch, seq, d = x.shape
    assert d == D_MODEL and ffn_w_gate.shape == (DEPTH, 2, D_MODEL, D_FF) and a_w_in.shape[0] == N_A_LAYERS
    assert a_w_in.shape[2] == A_MAIN + 2 * A_HEADS and kv_w.shape == (D_MODEL, 2 * B_KV_HEADS * B_DH)
    assert batch % DELTA_SEQS == 0 and seq % max(PROJ_ROWS, SWA_ROWS, CHUNK) == 0
    assert (batch * seq) % FFN_ROWS == 0
    x = x.reshape(batch * seq, d)
    ffn_w = lambda i, s: (ffn_norm[i, s], ffn_w_gate, ffn_w_up, ffn_w_down, (i, s))
    kv = None
    for i in range(DEPTH):
        x = _ffn(x, *ffn_w(i, 0))
        if i < N_A_LAYERS:
            main, ba = _a_in_proj(x, mix_norm[i], a_w_in[i], a_conv[i], seq)
            o = _delta_mixer_core(main, ba, a_A_log[i], a_dt_bias[i], a_o_norm[i], batch, seq)
            if i == N_A_LAYERS - 1:
                x, kv = _ffn(x, *ffn_w(i, 1), pre=(o, a_w_out[i]), post="kv", post_args=(kv_norm, kv_w, kv_b))
            else:
                x = _ffn(x, *ffn_w(i, 1), pre=(o, a_w_out[i]))
        else:
            j = i - N_A_LAYERS
            x = _swa_mixer(x, mix_norm[i], b_w_q[j], b_b_q[j], b_sinks[j], b_w_out[j], kv, batch, seq)
            if i == DEPTH - 1:
                x = _ffn(x, *ffn_w(i, 1), post="final", post_args=(final_norm,))
            else:
                x = _ffn(x, *ffn_w(i, 1))
    return x.reshape(batch, seq, d)
```
